```python
import jax, jax.numpy as jnp
from jax import lax
import numpy as np

D_MODEL = 1024
BATCH = 4
SEQ = 8192
DEPTH = 2
DEC_BATCH = 32
DEC_SEQ = 2048
PAST_LEN = 128

GRID_W = 64
N_HEADS = 8
HEAD_DIM = 64
ATTN_WIDTH = N_HEADS * HEAD_DIM
CONV_WIDTH = D_MODEL - ATTN_WIDTH
WIN_H_MAX = 8
WIN_W = 16
Q_COL_BLOCK = 16
K_COL_BLOCK = Q_COL_BLOCK + WIN_W
D_FF = 2816
PROJ_WIDTH = 3 * ATTN_WIDTH + 3 * CONV_WIDTH + 2 * D_MODEL
DEEPNORM_ALPHA = (2 * DEPTH) ** 0.25
DEEPNORM_BETA = (8 * DEPTH) ** -0.25
LN_EPS = 1e-5

kernel_name = 'hybrid_natten_shortconv_deepnorm_encoder'


def layer_norm(x, g, b):
    xf = x.astype(jnp.float32)
    mu = jnp.mean(xf, axis=-1, keepdims=True)
    var = jnp.mean(jnp.square(xf - mu), axis=-1, keepdims=True)
    y = (xf - mu) * lax.rsqrt(var + LN_EPS)
    return (y * g.astype(jnp.float32) + b.astype(jnp.float32)).astype(x.dtype)


def dwconv3(x, w, b):
    xp = jnp.pad(x, ((0, 0), (1, 1), (0, 0)))
    return xp[:, :-2] * w[0] + xp[:, 1:-1] * w[1] + xp[:, 2:] * w[2] + b


def neighbourhood_attention(q, k, v, rpb):
    B, T, H, dh = q.shape
    rows = T // GRID_W
    kh = min(WIN_H_MAX, rows)
    scale = dh ** -0.5
    qg = q.reshape(B, rows, GRID_W, H, dh)
    kg = k.reshape(B, rows, GRID_W, H, dh)
    vg = v.reshape(B, rows, GRID_W, H, dh)
    row_starts = jnp.clip(jnp.arange(rows) - kh // 2, 0, rows - kh)

    def one_row(args):
        i, rs = args
        q_i = lax.dynamic_index_in_dim(qg, i, axis=1, keepdims=False)
        k_i = lax.dynamic_slice_in_dim(kg, rs, kh, axis=1)
        v_i = lax.dynamic_slice_in_dim(vg, rs, kh, axis=1)
        dr_idx = rs + jnp.arange(kh) - i + (WIN_H_MAX - 1)
        outs = []
        for c in range(GRID_W // Q_COL_BLOCK):
            q0 = c * Q_COL_BLOCK
            cs = min(max(q0 - WIN_W // 2, 0), GRID_W - K_COL_BLOCK)
            qj = jnp.arange(q0, q0 + Q_COL_BLOCK)
            kj = jnp.arange(cs, cs + K_COL_BLOCK)
            js = jnp.clip(qj - WIN_W // 2, 0, GRID_W - WIN_W)
            valid = (kj[None, :] >= js[:, None]) & (kj[None, :] < js[:, None] + WIN_W)
            dc_idx = jnp.clip(kj[None, :] - qj[:, None] + (WIN_W - 1), 0, 2 * WIN_W - 2)
            bias = rpb[:, dr_idx[None, :, None], dc_idx[:, None, :]]
            qc = q_i[:, q0:q0 + Q_COL_BLOCK]
            kc = k_i[:, :, cs:cs + K_COL_BLOCK]
            vc = v_i[:, :, cs:cs + K_COL_BLOCK]
            s = jnp.einsum('bqhd,brkhd->bhqrk', qc, kc).astype(jnp.float32) * scale
            s = s + bias.astype(jnp.float32)
            s = jnp.where(valid[:, None, :], s, -jnp.inf)
            p = jax.nn.softmax(s.reshape(B, H, Q_COL_BLOCK, kh * K_COL_BLOCK), axis=-1)
            p = p.astype(v.dtype).reshape(B, H, Q_COL_BLOCK, kh, K_COL_BLOCK)
            outs.append(jnp.einsum('bhqrk,brkhd->bqhd', p, vc))
        return jnp.concatenate(outs, axis=1)

    o = lax.map(one_row, (jnp.arange(rows), row_starts))
    return jnp.moveaxis(o, 0, 1).reshape(B, T, H * dh)


def encoder_layer(x, w_in, b_in, attn_rpb, sc_conv_w, sc_conv_b, w_br_attn, w_br_conv,
                  w_o, b_o, ln1_g, ln1_b, ffn_w_up, ffn_b_up, ffn_conv_w, ffn_conv_b,
                  ffn_w_down, ffn_b_down, ln2_g, ln2_b):
    B, T, _ = x.shape
    proj = x @ w_in + b_in
    splits = np.cumsum([ATTN_WIDTH, ATTN_WIDTH, ATTN_WIDTH, CONV_WIDTH, CONV_WIDTH,
                        CONV_WIDTH, D_MODEL]).tolist()
    q, k, v, u, gb_in, gc_in, gate_a, gate_c = jnp.split(proj, splits, axis=-1)
    ya = neighbourhood_attention(q.reshape(B, T, N_HEADS, HEAD_DIM),
                                 k.reshape(B, T, N_HEADS, HEAD_DIM),
                                 v.reshape(B, T, N_HEADS, HEAD_DIM), attn_rpb)
    yc = gb_in * dwconv3(gc_in * u, sc_conv_w, sc_conv_b)
    merged = jax.nn.sigmoid(gate_a) * (ya @ w_br_attn) + jax.nn.sigmoid(gate_c) * (yc @ w_br_conv)
    mix = merged @ w_o + b_o
    x = layer_norm(DEEPNORM_ALPHA * x + mix, ln1_g, ln1_b)
    h = dwconv3(x @ ffn_w_up + ffn_b_up, ffn_conv_w, ffn_conv_b)
    h_gate, h_val = jnp.split(h, 2, axis=-1)
    f = (jax.nn.gelu(h_gate) * h_val) @ ffn_w_down + ffn_b_down
    return layer_norm(DEEPNORM_ALPHA * x + f, ln2_g, ln2_b)


def trunk(x, w_in, b_in, attn_rpb, sc_conv_w, sc_conv_b, w_br_attn, w_br_conv, w_o, b_o,
          ln1_g, ln1_b, ffn_w_up, ffn_b_up, ffn_conv_w, ffn_conv_b, ffn_w_down, ffn_b_down,
          ln2_g, ln2_b):
    for l in range(DEPTH):
        x = encoder_layer(x, w_in[l], b_in[l], attn_rpb[l], sc_conv_w[l], sc_conv_b[l],
                          w_br_attn[l], w_br_conv[l], w_o[l], b_o[l], ln1_g[l], ln1_b[l],
                          ffn_w_up[l], ffn_b_up[l], ffn_conv_w[l], ffn_conv_b[l],
                          ffn_w_down[l], ffn_b_down[l], ln2_g[l], ln2_b[l])
    return x


def setup_inputs(seed: int = 0) -> dict:
    key = jax.random.key(seed)
    ks = jax.random.split(key, 22)
    nrm = lambda k, s: jax.random.normal(k, s, dtype=jnp.float32)
    beta = DEEPNORM_BETA
    col_scale = jnp.concatenate([
        jnp.ones((2 * ATTN_WIDTH,), jnp.float32),
        jnp.full((ATTN_WIDTH,), beta, jnp.float32),
        jnp.ones((3 * CONV_WIDTH + 2 * D_MODEL,), jnp.float32)])
    return {
        'x_prompt': nrm(ks[0], (BATCH, SEQ, D_MODEL)),
        'x_sample': nrm(ks[1], (DEC_BATCH, DEC_SEQ, D_MODEL)),
        'w_in': nrm(ks[2], (DEPTH, D_MODEL, PROJ_WIDTH)) * (D_MODEL ** -0.5) * col_scale,
        'b_in': 0.02 * nrm(ks[3], (DEPTH, PROJ_WIDTH)),
        'attn_rpb': 0.02 * nrm(ks[4], (DEPTH, N_HEADS, 2 * WIN_H_MAX - 1, 2 * WIN_W - 1)),
        'sc_conv_w': nrm(ks[5], (DEPTH, 3, CONV_WIDTH)) * (3 ** -0.5),
        'sc_conv_b': 0.02 * nrm(ks[6], (DEPTH, CONV_WIDTH)),
        'w_br_attn': nrm(ks[7], (DEPTH, ATTN_WIDTH, D_MODEL)) * (ATTN_WIDTH ** -0.5) * beta,
        'w_br_conv': nrm(ks[8], (DEPTH, CONV_WIDTH, D_MODEL)) * (CONV_WIDTH ** -0.5) * beta,
        'w_o': nrm(ks[9], (DEPTH, D_MODEL, D_MODEL)) * (D_MODEL ** -0.5) * beta,
        'b_o': 0.02 * nrm(ks[10], (DEPTH, D_MODEL)),
        'ln1_g': 1.0 + 0.01 * nrm(ks[11], (DEPTH, D_MODEL)),
        'ln1_b': 0.01 * nrm(ks[12], (DEPTH, D_MODEL)),
        'ffn_w_up': nrm(ks[13], (DEPTH, D_MODEL, 2 * D_FF)) * (D_MODEL ** -0.5) * beta,
        'ffn_b_up': 0.02 * nrm(ks[14], (DEPTH, 2 * D_FF)),
        'ffn_conv_w': nrm(ks[15], (DEPTH, 3, 2 * D_FF)) * (3 ** -0.5),
        'ffn_conv_b': 0.02 * nrm(ks[16], (DEPTH, 2 * D_FF)),
        'ffn_w_down': nrm(ks[17], (DEPTH, D_FF, D_MODEL)) * (D_FF ** -0.5) * beta,
        'ffn_b_down': 0.02 * nrm(ks[18], (DEPTH, D_MODEL)),
        'ln2_g': 1.0 + 0.01 * nrm(ks[19], (DEPTH, D_MODEL)),
        'ln2_b': 0.01 * nrm(ks[20], (DEPTH, D_MODEL)),
    }


def reference(x_prompt, x_sample, w_in, b_in, attn_rpb, sc_conv_w, sc_conv_b, w_br_attn,
              w_br_conv, w_o, b_o, ln1_g, ln1_b, ffn_w_up, ffn_b_up, ffn_conv_w, ffn_conv_b,
              ffn_w_down, ffn_b_down, ln2_g, ln2_b):
    y_prompt = trunk(x_prompt, w_in, b_in, attn_rpb, sc_conv_w, sc_conv_b, w_br_attn,
                     w_br_conv, w_o, b_o, ln1_g, ln1_b, ffn_w_up, ffn_b_up, ffn_conv_w,
                     ffn_conv_b, ffn_w_down, ffn_b_down, ln2_g, ln2_b)
    y_sample = trunk(x_sample, w_in, b_in, attn_rpb, sc_conv_w, sc_conv_b, w_br_attn,
                     w_br_conv, w_o, b_o, ln1_g, ln1_b, ffn_w_up, ffn_b_up, ffn_conv_w,
                     ffn_conv_b, ffn_w_down, ffn_b_down, ln2_g, ln2_b)
    return (y_prompt, y_sample)
```

```python
import functools

import jax
import jax.numpy as jnp
import numpy as np
from jax import lax
from jax.experimental import pallas as pl
from jax.experimental.pallas import tpu as pltpu

D_MODEL = 1024
DEPTH = 2
GRID_W = 64
N_HEADS = 8
HEAD_DIM = 64
ATTN_WIDTH = N_HEADS * HEAD_DIM
CONV_WIDTH = D_MODEL - ATTN_WIDTH
WIN_H = 8
WIN_W = 16
D_FF = 2816
PROJ_WIDTH = 3 * ATTN_WIDTH + 3 * CONV_WIDTH + 2 * D_MODEL
DEEPNORM_ALPHA = (2 * DEPTH) ** 0.25
LN_EPS = 1e-5
ATTN_SCALE = HEAD_DIM ** -0.5

HALO = 8
TOKEN_TILE = 512
FFN_CHUNK = 256
KEY_TOKENS = WIN_H * GRID_W
VMEM_LIMIT = 56 * 1024 * 1024

BF16 = jnp.bfloat16
F32 = jnp.float32


def _dot(a, b):
    return jnp.dot(a, b, preferred_element_type=F32)


def _layer_norm(y, g, b):
    mu = jnp.mean(y, axis=-1, keepdims=True)
    yc = y - mu
    var = jnp.mean(yc * yc, axis=-1, keepdims=True)
    return yc * lax.rsqrt(var + LN_EPS) * g + b


def _seq_edge_flags(tile_idx, tiles_per_seq):
    pos = tile_idx % tiles_per_seq
    return pos == 0, pos == tiles_per_seq - 1


def _halo_rows_valid(n_ext, first, last):
    r = lax.broadcasted_iota(jnp.int32, (n_ext, 1), 0)
    bad = (first & (r < HALO)) | (last & (r >= n_ext - HALO))
    return jnp.logical_not(bad)


def _dwconv3_rows(h, w_ref, b_ref):
    n = h.shape[0]
    up = pltpu.roll(h, 1, 0)
    dn = pltpu.roll(h, n - 1, 0)
    return up * w_ref[0:1, :] + h * w_ref[1:2, :] + dn * w_ref[2:3, :] + b_ref[...]


def _proj_kernel(tiles_per_seq, xm_ref, xp_ref, xn_ref, w_ref, b_ref, cw_ref, cb_ref,
                 q_ref, k_ref, v_ref, yc_ref, ga_ref, gc_ref):
    tm = xm_ref.shape[0]
    first, last = _seq_edge_flags(pl.program_id(0), tiles_per_seq)
    xm32 = xm_ref[...]
    xm = xm32.astype(BF16)
    a = ATTN_WIDTH
    c = CONV_WIDTH
    qkv = _dot(xm, w_ref[:, 0:3 * a]) + b_ref[:, 0:3 * a]
    q_ref[...] = (qkv[:, 0:a] * ATTN_SCALE).astype(BF16)
    k_ref[...] = qkv[:, a:2 * a].astype(BF16)
    v_ref[...] = qkv[:, 2 * a:3 * a].astype(BF16)
    xe = jnp.concatenate([xp_ref[...], xm32, xn_ref[...]], axis=0).astype(BF16)
    o = 3 * a
    ugc = _dot(xe, w_ref[:, o:o + 3 * c]) + b_ref[:, o:o + 3 * c]
    u = ugc[:, 0:c]
    gb = ugc[:, c:2 * c]
    gcin = ugc[:, 2 * c:3 * c]
    cu = jnp.where(_halo_rows_valid(tm + 2 * HALO, first, last), gcin * u, 0.0)
    conv = _dwconv3_rows(cu, cw_ref, cb_ref)
    yc_ref[...] = (gb * conv)[HALO:HALO + tm, :].astype(BF16)
    o = 3 * a + 3 * c
    g = _dot(xm, w_ref[:, o:o + 2 * D_MODEL]) + b_ref[:, o:o + 2 * D_MODEL]
    g = jax.nn.sigmoid(g)
    ga_ref[...] = g[:, 0:D_MODEL].astype(BF16)
    gc_ref[...] = g[:, D_MODEL:2 * D_MODEL].astype(BF16)


def _const_spec(shape):
    return pl.BlockSpec(shape, lambda *_: (0,) * len(shape), pipeline_mode=pl.Buffered(1))


def _halo_specs(tm, n_tokens, width):
    per = tm // HALO
    nblk = n_tokens // HALO
    return [
        pl.BlockSpec((tm, width), lambda i: (i, 0)),
        pl.BlockSpec((HALO, width), lambda i: (jnp.maximum(i * per - 1, 0), 0)),
        pl.BlockSpec((HALO, width), lambda i: (jnp.minimum((i + 1) * per, nblk - 1), 0)),
    ]


def _proj(x, seq, w_in, b_in, cw, cb):
    n = x.shape[0]
    tm = TOKEN_TILE
    tok = lambda width: pl.BlockSpec((tm, width), lambda i: (i, 0))
    return pl.pallas_call(
        functools.partial(_proj_kernel, seq // tm),
        grid=(n // tm,),
        in_specs=_halo_specs(tm, n, D_MODEL) + [
            _const_spec((D_MODEL, PROJ_WIDTH)), _const_spec((1, PROJ_WIDTH)),
            _const_spec((3, CONV_WIDTH)), _const_spec((1, CONV_WIDTH))],
        out_specs=[tok(ATTN_WIDTH)] * 3 + [tok(CONV_WIDTH)] + [tok(D_MODEL)] * 2,
        out_shape=[jax.ShapeDtypeStruct((n, ATTN_WIDTH), BF16)] * 3
        + [jax.ShapeDtypeStruct((n, CONV_WIDTH), BF16)]
        + [jax.ShapeDtypeStruct((n, D_MODEL), BF16)] * 2,
        compiler_params=pltpu.CompilerParams(
            dimension_semantics=("parallel",), vmem_limit_bytes=VMEM_LIMIT),
        name="proj",
    )(x, x, x, w_in, b_in, cw, cb)


def _attn_kernel(rows, q_ref, k_ref, v_ref, bias_ref, o_ref):
    r = pl.program_id(1)
    rs = jnp.clip(r - WIN_H // 2, 0, rows - WIN_H)
    start = pl.multiple_of(rs * GRID_W, GRID_W)
    kw = k_ref[pl.ds(start, KEY_TOKENS), :]
    vw = v_ref[pl.ds(start, KEY_TOKENS), :]
    lane = lax.broadcasted_iota(jnp.int32, (GRID_W, 128), 1)
    low = lane < HEAD_DIM
    for p in range(N_HEADS // 2):
        sl = slice(p * 128, (p + 1) * 128)
        qp = q_ref[:, sl]
        kp = kw[:, sl]
        vp = vw[:, sl]
        outs = []
        for hh in range(2):
            qm = jnp.where(low if hh == 0 else jnp.logical_not(low), qp, jnp.zeros_like(qp))
            s = lax.dot_general(qm, kp, (((1,), (1,)), ((), ())), preferred_element_type=F32)
            s = s + bias_ref[0, 2 * p + hh]
            m = jnp.max(s, axis=-1, keepdims=True)
            e = jnp.exp(s - m)
            l = jnp.sum(e, axis=-1, keepdims=True)
            outs.append(_dot(e.astype(BF16), vp) / l)
        o_ref[:, sl] = jnp.where(low, outs[0], outs[1]).astype(BF16)


def _attn(q, k, v, bias, batch, seq):
    rows = seq // GRID_W
    n = q.shape[0]

    def bias_idx(b, r):
        return (r - jnp.clip(r - WIN_H // 2, 0, rows - WIN_H), 0, 0, 0)

    return pl.pallas_call(
        functools.partial(_attn_kernel, rows),
        grid=(batch, rows),
        in_specs=[
            pl.BlockSpec((GRID_W, ATTN_WIDTH), lambda b, r: (b * rows + r, 0)),
            pl.BlockSpec((seq, ATTN_WIDTH), lambda b, r: (b, 0)),
            pl.BlockSpec((seq, ATTN_WIDTH), lambda b, r: (b, 0)),
            pl.BlockSpec((1, N_HEADS, GRID_W, KEY_TOKENS), bias_idx),
        ],
        out_specs=pl.BlockSpec((GRID_W, ATTN_WIDTH), lambda b, r: (b * rows + r, 0)),
        out_shape=jax.ShapeDtypeStruct((n, ATTN_WIDTH), BF16),
        compiler_params=pltpu.CompilerParams(
            dimension_semantics=("parallel", "arbitrary"), vmem_limit_bytes=VMEM_LIMIT),
        name="attn",
    )(q, k, v, bias)


def _attn_bias_table(rpb):
    d = np.arange(WIN_H)[:, None]
    r = np.arange(WIN_H)[None, :]
    dr_idx = r - d + (WIN_H - 1)
    qj = np.arange(GRID_W)[:, None]
    kj = np.arange(GRID_W)[None, :]
    dc_idx = np.clip(kj - qj + (WIN_W - 1), 0, 2 * WIN_W - 2)
    js = np.clip(qj - WIN_W // 2, 0, GRID_W - WIN_W)
    valid = (kj >= js) & (kj < js + WIN_W)
    b = rpb[:, dr_idx[:, :, None, None], dc_idx[None, None, :, :]]
    b = jnp.where(valid[None, None, None], b.astype(F32), -jnp.inf)
    b = jnp.transpose(b, (1, 0, 3, 2, 4))
    return b.reshape(WIN_H, N_HEADS, GRID_W, KEY_TOKENS)


def _merge_kernel(x_ref, ya_ref, yc_ref, ga_ref, gc_ref, wa_ref, wc_ref, wo_ref, bo_ref,
                  g_ref, b_ref, o_ref):
    merged = (ga_ref[...].astype(F32) * _dot(ya_ref[...], wa_ref[...])
              + gc_ref[...].astype(F32) * _dot(yc_ref[...], wc_ref[...]))
    mix = _dot(merged.astype(BF16), wo_ref[...]) + bo_ref[...]
    y = DEEPNORM_ALPHA * x_ref[...] + mix
    o_ref[...] = _layer_norm(y, g_ref[...], b_ref[...])


def _merge(x, ya, yc, ga, gc, wa, wc, wo, bo, g, b):
    n = x.shape[0]
    tm = TOKEN_TILE
    tok = lambda width: pl.BlockSpec((tm, width), lambda i: (i, 0))
    vec = _const_spec((1, D_MODEL))
    return pl.pallas_call(
        _merge_kernel,
        grid=(n // tm,),
        in_specs=[tok(D_MODEL), tok(ATTN_WIDTH), tok(CONV_WIDTH), tok(D_MODEL), tok(D_MODEL),
                  _const_spec((ATTN_WIDTH, D_MODEL)), _const_spec((CONV_WIDTH, D_MODEL)),
                  _const_spec((D_MODEL, D_MODEL)), vec, vec, vec],
        out_specs=tok(D_MODEL),
        out_shape=jax.ShapeDtypeStruct((n, D_MODEL), F32),
        compiler_params=pltpu.CompilerParams(
            dimension_semantics=("parallel",), vmem_limit_bytes=VMEM_LIMIT),
        name="merge",
    )(x, ya, yc, ga, gc, wa, wc, wo, bo, g, b)


def _ffn_kernel(tiles_per_seq, xm_ref, xp_ref, xn_ref, wu_ref, bu_ref, cw_ref, cb_ref,
                wd_ref, bd_ref, g_ref, b_ref, o_ref, act_ref):
    tm = xm_ref.shape[0]
    first, last = _seq_edge_flags(pl.program_id(0), tiles_per_seq)
    xm32 = xm_ref[...]
    xe = jnp.concatenate([xp_ref[...], xm32, xn_ref[...]], axis=0).astype(BF16)
    valid = _halo_rows_valid(tm + 2 * HALO, first, last)

    def conv_half(c0):
        sl = slice(c0, c0 + FFN_CHUNK)
        h = jnp.where(valid, _dot(xe, wu_ref[:, sl]) + bu_ref[:, sl], 0.0)
        return _dwconv3_rows(h, cw_ref.at[:, sl], cb_ref.at[:, sl])[HALO:HALO + tm, :]

    for j in range(D_FF // FFN_CHUNK):
        c0 = j * FFN_CHUNK
        a = jax.nn.gelu(conv_half(c0)) * conv_half(D_FF + c0)
        act_ref[:, c0:c0 + FFN_CHUNK] = a.astype(BF16)
    f = _dot(act_ref[...], wd_ref[...]) + bd_ref[...]
    y = DEEPNORM_ALPHA * xm32 + f
    o_ref[...] = _layer_norm(y, g_ref[...], b_ref[...])


def _ffn(x, seq, wu, bu, cw, cb, wd, bd, g, b):
    n = x.shape[0]
    tm = TOKEN_TILE
    vec = _const_spec((1, D_MODEL))
    return pl.pallas_call(
        functools.partial(_ffn_kernel, seq // tm),
        grid=(n // tm,),
        in_specs=_halo_specs(tm, n, D_MODEL) + [
            _const_spec((D_MODEL, 2 * D_FF)), _const_spec((1, 2 * D_FF)),
            _const_spec((3, 2 * D_FF)), _const_spec((1, 2 * D_FF)),
            _const_spec((D_FF, D_MODEL)), vec, vec, vec],
        out_specs=pl.BlockSpec((tm, D_MODEL), lambda i: (i, 0)),
        out_shape=jax.ShapeDtypeStruct((n, D_MODEL), F32),
        scratch_shapes=[pltpu.VMEM((tm, D_FF), BF16)],
        compiler_params=pltpu.CompilerParams(
            dimension_semantics=("parallel",), vmem_limit_bytes=VMEM_LIMIT),
        name="ffn",
    )(x, x, x, wu, bu, cw, cb, wd, bd, g, b)


def _trunk(x3, layers):
    batch, seq, _ = x3.shape
    assert seq % TOKEN_TILE == 0 and seq % GRID_W == 0 and seq // GRID_W >= WIN_H
    x = x3.reshape(batch * seq, D_MODEL)
    for p in layers:
        q, k, v, yc, ga, gc = _proj(x, seq, p["w_in"], p["b_in"], p["sc_w"], p["sc_b"])
        ya = _attn(q, k, v, p["bias"], batch, seq)
        x = _merge(x, ya, yc, ga, gc, p["w_br_attn"], p["w_br_conv"], p["w_o"], p["b_o"],
                   p["ln1_g"], p["ln1_b"])
        x = _ffn(x, seq, p["w_up"], p["b_up"], p["f_w"], p["f_b"], p["w_down"], p["b_down"],
                 p["ln2_g"], p["ln2_b"])
    return x.reshape(batch, seq, D_MODEL)


def kernel(x_prompt, x_sample, w_in, b_in, attn_rpb, sc_conv_w, sc_conv_b, w_br_attn, w_br_conv,
           w_o, b_o, ln1_g, ln1_b, ffn_w_up, ffn_b_up, ffn_conv_w, ffn_conv_b, ffn_w_down,
           ffn_b_down, ln2_g, ln2_b):
    row = lambda a: a.reshape(1, -1).astype(F32)
    layers = []
    for l in range(DEPTH):
        layers.append(dict(
            w_in=w_in[l].astype(BF16), b_in=row(b_in[l]),
            sc_w=sc_conv_w[l].astype(F32), sc_b=row(sc_conv_b[l]),
            bias=_attn_bias_table(attn_rpb[l]),
            w_br_attn=w_br_attn[l].astype(BF16), w_br_conv=w_br_conv[l].astype(BF16),
            w_o=w_o[l].astype(BF16), b_o=row(b_o[l]),
            ln1_g=row(ln1_g[l]), ln1_b=row(ln1_b[l]),
            w_up=ffn_w_up[l].astype(BF16), b_up=row(ffn_b_up[l]),
            f_w=ffn_conv_w[l].astype(F32), f_b=row(ffn_conv_b[l]),
            w_down=ffn_w_down[l].astype(BF16), b_down=row(ffn_b_down[l]),
            ln2_g=row(ln2_g[l]), ln2_b=row(ln2_b[l])))
    return _trunk(x_prompt, layers), _trunk(x_sample, layers)
```

```python
import functools

import jax
import jax.numpy as jnp
import numpy as np
from jax import lax
from jax.experimental import pallas as pl
from jax.experimental.pallas import tpu as pltpu

D_MODEL = 1024
DEPTH = 2
GRID_W = 64
N_HEADS = 8
HEAD_DIM = 64
ATTN_WIDTH = N_HEADS * HEAD_DIM
CONV_WIDTH = D_MODEL - ATTN_WIDTH
WIN_H = 8
WIN_W = 16
D_FF = 2816
PROJ_WIDTH = 3 * ATTN_WIDTH + 3 * CONV_WIDTH + 2 * D_MODEL
DEEPNORM_ALPHA = (2 * DEPTH) ** 0.25
LN_EPS = 1e-5
ATTN_SCALE = HEAD_DIM ** -0.5

HALO = 8
TOKEN_TILE = 512
FFN_CHUNK = 256
KEY_TOKENS = WIN_H * GRID_W
PAIR_LANES = 2 * HEAD_DIM
N_PAIRS = N_HEADS // 2
ATTN_ROWS_PER_STEP = 8
VMEM_LIMIT = 56 * 1024 * 1024

BF16 = jnp.bfloat16
F32 = jnp.float32


def _dot(a, b):
    return jnp.dot(a, b, preferred_element_type=F32)


def _layer_norm(y, g, b):
    mu = jnp.mean(y, axis=-1, keepdims=True)
    yc = y - mu
    var = jnp.mean(yc * yc, axis=-1, keepdims=True)
    return yc * lax.rsqrt(var + LN_EPS) * g + b


def _seq_edge_flags(tile_idx, tiles_per_seq):
    pos = tile_idx % tiles_per_seq
    return pos == 0, pos == tiles_per_seq - 1


def _halo_rows_valid(n_ext, first, last):
    r = lax.broadcasted_iota(jnp.int32, (n_ext, 1), 0)
    bad = (first & (r < HALO)) | (last & (r >= n_ext - HALO))
    return jnp.logical_not(bad)


def _dwconv3_rows(h, w_ref, b_ref):
    n = h.shape[0]
    up = pltpu.roll(h, 1, 0)
    dn = pltpu.roll(h, n - 1, 0)
    return up * w_ref[0:1, :] + h * w_ref[1:2, :] + dn * w_ref[2:3, :] + b_ref[...]


def _proj_kernel(tiles_per_seq, xm_ref, xp_ref, xn_ref, w_ref, b_ref, cw_ref, cb_ref,
                 q_ref, k_ref, v_ref, yc_ref, ga_ref, gc_ref):
    tm = xm_ref.shape[0]
    first, last = _seq_edge_flags(pl.program_id(0), tiles_per_seq)
    xm32 = xm_ref[...]
    xm = xm32.astype(BF16)
    a = ATTN_WIDTH
    c = CONV_WIDTH
    qkv = _dot(xm, w_ref[:, 0:3 * a]) + b_ref[:, 0:3 * a]
    q_ref[...] = (qkv[:, 0:a] * ATTN_SCALE).astype(BF16)
    k_ref[...] = qkv[:, a:2 * a].astype(BF16)
    v_ref[...] = qkv[:, 2 * a:3 * a].astype(BF16)
    xe = jnp.concatenate([xp_ref[...], xm32, xn_ref[...]], axis=0).astype(BF16)
    o = 3 * a
    ugc = _dot(xe, w_ref[:, o:o + 3 * c]) + b_ref[:, o:o + 3 * c]
    u = ugc[:, 0:c]
    gb = ugc[:, c:2 * c]
    gcin = ugc[:, 2 * c:3 * c]
    cu = jnp.where(_halo_rows_valid(tm + 2 * HALO, first, last), gcin * u, 0.0)
    conv = _dwconv3_rows(cu, cw_ref, cb_ref)
    yc_ref[...] = (gb * conv)[HALO:HALO + tm, :].astype(BF16)
    o = 3 * a + 3 * c
    g = _dot(xm, w_ref[:, o:o + 2 * D_MODEL]) + b_ref[:, o:o + 2 * D_MODEL]
    g = jax.nn.sigmoid(g)
    ga_ref[...] = g[:, 0:D_MODEL].astype(BF16)
    gc_ref[...] = g[:, D_MODEL:2 * D_MODEL].astype(BF16)


def _const_spec(shape):
    return pl.BlockSpec(shape, lambda *_: (0,) * len(shape), pipeline_mode=pl.Buffered(1))


def _halo_specs(tm, n_tokens, width):
    per = tm // HALO
    nblk = n_tokens // HALO
    return [
        pl.BlockSpec((tm, width), lambda i: (i, 0)),
        pl.BlockSpec((HALO, width), lambda i: (jnp.maximum(i * per - 1, 0), 0)),
        pl.BlockSpec((HALO, width), lambda i: (jnp.minimum((i + 1) * per, nblk - 1), 0)),
    ]


def _proj(x, seq, w_in, b_in, cw, cb):
    n = x.shape[0]
    tm = TOKEN_TILE
    tok = lambda width: pl.BlockSpec((tm, width), lambda i: (i, 0))
    return pl.pallas_call(
        functools.partial(_proj_kernel, seq // tm),
        grid=(n // tm,),
        in_specs=_halo_specs(tm, n, D_MODEL) + [
            _const_spec((D_MODEL, PROJ_WIDTH)), _const_spec((1, PROJ_WIDTH)),
            _const_spec((3, CONV_WIDTH)), _const_spec((1, CONV_WIDTH))],
        out_specs=[tok(ATTN_WIDTH)] * 3 + [tok(CONV_WIDTH)] + [tok(D_MODEL)] * 2,
        out_shape=[jax.ShapeDtypeStruct((n, ATTN_WIDTH), BF16)] * 3
        + [jax.ShapeDtypeStruct((n, CONV_WIDTH), BF16)]
        + [jax.ShapeDtypeStruct((n, D_MODEL), BF16)] * 2,
        compiler_params=pltpu.CompilerParams(
            dimension_semantics=("parallel",), vmem_limit_bytes=VMEM_LIMIT),
        name="proj",
    )(x, x, x, w_in, b_in, cw, cb)


def _attn_row(rows, r, q_ref, k_ref, v_ref, bias_ref, o_ref, q_off):
    rs = jnp.clip(r - WIN_H // 2, 0, rows - WIN_H)
    d = r - rs
    start = pl.multiple_of(rs * GRID_W, GRID_W)
    low = lax.broadcasted_iota(jnp.int32, (GRID_W, PAIR_LANES), 1) < HEAD_DIM
    ones = jnp.ones((KEY_TOKENS, PAIR_LANES), BF16)
    qrow = pl.ds(q_off, GRID_W)
    scores = []
    for p in range(N_PAIRS):
        sl = slice(p * PAIR_LANES, (p + 1) * PAIR_LANES)
        qp = q_ref[qrow, sl]
        zero = jnp.zeros_like(qp)
        qm = jnp.concatenate([jnp.where(low, qp, zero), jnp.where(low, zero, qp)], axis=0)
        kp = k_ref[pl.ds(start, KEY_TOKENS), sl]
        s = lax.dot_general(qm, kp, (((1,), (1,)), ((), ())), preferred_element_type=F32)
        scores.append(s + bias_ref[d, p])
    probs = []
    for s in scores:
        m = jnp.max(s, axis=-1, keepdims=True)
        probs.append(jnp.exp(s - m).astype(BF16))
    for p in range(N_PAIRS):
        sl = slice(p * PAIR_LANES, (p + 1) * PAIR_LANES)
        vext = jnp.concatenate([v_ref[pl.ds(start, KEY_TOKENS), sl], ones], axis=1)
        ol = _dot(probs[p], vext)
        num = jnp.where(low, ol[0:GRID_W, 0:PAIR_LANES], ol[GRID_W:, 0:PAIR_LANES])
        den = jnp.where(low, ol[0:GRID_W, PAIR_LANES:], ol[GRID_W:, PAIR_LANES:])
        o_ref[qrow, sl] = (num / den).astype(BF16)


def _attn_kernel(rows, q_ref, k_ref, v_ref, bias_ref, o_ref):
    r0 = pl.program_id(1) * ATTN_ROWS_PER_STEP

    def body(j, carry):
        _attn_row(rows, r0 + j, q_ref, k_ref, v_ref, bias_ref, o_ref,
                  pl.multiple_of(j * GRID_W, GRID_W))
        return carry

    lax.fori_loop(0, ATTN_ROWS_PER_STEP, body, 0, unroll=4)


def _attn(q, k, v, bias, batch, seq):
    rows = seq // GRID_W
    n = q.shape[0]
    rb = ATTN_ROWS_PER_STEP
    blk = rows // rb
    qo_spec = pl.BlockSpec((rb * GRID_W, ATTN_WIDTH), lambda b, i: (b * blk + i, 0))
    kv_spec = pl.BlockSpec((seq, ATTN_WIDTH), lambda b, i: (b, 0))
    return pl.pallas_call(
        functools.partial(_attn_kernel, rows),
        grid=(batch, blk),
        in_specs=[qo_spec, kv_spec, kv_spec,
                  _const_spec((WIN_H, N_PAIRS, 2 * GRID_W, KEY_TOKENS))],
        out_specs=qo_spec,
        out_shape=jax.ShapeDtypeStruct((n, ATTN_WIDTH), BF16),
        compiler_params=pltpu.CompilerParams(
            dimension_semantics=("parallel", "arbitrary"), vmem_limit_bytes=VMEM_LIMIT),
        name="attn",
    )(q, k, v, bias)


def _attn_bias_table(rpb):
    qj = np.arange(GRID_W)[:, None]
    kj = np.arange(GRID_W)[None, :]
    dc_idx = np.clip(kj - qj + (WIN_W - 1), 0, 2 * WIN_W - 2)
    js = np.clip(qj - WIN_W // 2, 0, GRID_W - WIN_W)
    valid = (kj >= js) & (kj < js + WIN_W)
    onehot = (dc_idx[None] == np.arange(2 * WIN_W - 1)[:, None, None]).astype(np.float32)
    c = jnp.einsum("hdc,cqk->hqdk", rpb.astype(F32), onehot, precision=lax.Precision.HIGHEST)
    c = jnp.where(valid[None, :, None, :], c, -jnp.inf)
    tabs = [c[:, :, WIN_H - 1 - d:2 * WIN_H - 1 - d, :].reshape(N_HEADS, GRID_W, KEY_TOKENS)
            for d in range(WIN_H)]
    return jnp.stack(tabs).reshape(WIN_H, N_PAIRS, 2 * GRID_W, KEY_TOKENS)


def _merge_kernel(x_ref, ya_ref, yc_ref, ga_ref, gc_ref, wa_ref, wc_ref, wo_ref, bo_ref,
                  g_ref, b_ref, o_ref):
    merged = (ga_ref[...].astype(F32) * _dot(ya_ref[...], wa_ref[...])
              + gc_ref[...].astype(F32) * _dot(yc_ref[...], wc_ref[...]))
    mix = _dot(merged.astype(BF16), wo_ref[...]) + bo_ref[...]
    y = DEEPNORM_ALPHA * x_ref[...] + mix
    o_ref[...] = _layer_norm(y, g_ref[...], b_ref[...])


def _merge(x, ya, yc, ga, gc, wa, wc, wo, bo, g, b):
    n = x.shape[0]
    tm = TOKEN_TILE
    tok = lambda width: pl.BlockSpec((tm, width), lambda i: (i, 0))
    vec = _const_spec((1, D_MODEL))
    return pl.pallas_call(
        _merge_kernel,
        grid=(n // tm,),
        in_specs=[tok(D_MODEL), tok(ATTN_WIDTH), tok(CONV_WIDTH), tok(D_MODEL), tok(D_MODEL),
                  _const_spec((ATTN_WIDTH, D_MODEL)), _const_spec((CONV_WIDTH, D_MODEL)),
                  _const_spec((D_MODEL, D_MODEL)), vec, vec, vec],
        out_specs=tok(D_MODEL),
        out_shape=jax.ShapeDtypeStruct((n, D_MODEL), F32),
        compiler_params=pltpu.CompilerParams(
            dimension_semantics=("parallel",), vmem_limit_bytes=VMEM_LIMIT),
        name="merge",
    )(x, ya, yc, ga, gc, wa, wc, wo, bo, g, b)


def _ffn_kernel(tiles_per_seq, xm_ref, xp_ref, xn_ref, wu_ref, bu_ref, cw_ref, cb_ref,
                wd_ref, bd_ref, g_ref, b_ref, o_ref, act_ref):
    tm = xm_ref.shape[0]
    first, last = _seq_edge_flags(pl.program_id(0), tiles_per_seq)
    xm32 = xm_ref[...]
    xe = jnp.concatenate([xp_ref[...], xm32, xn_ref[...]], axis=0).astype(BF16)
    valid = _halo_rows_valid(tm + 2 * HALO, first, last)

    def conv_half(c0):
        sl = slice(c0, c0 + FFN_CHUNK)
        h = jnp.where(valid, _dot(xe, wu_ref[:, sl]) + bu_ref[:, sl], 0.0)
        return _dwconv3_rows(h, cw_ref.at[:, sl], cb_ref.at[:, sl])[HALO:HALO + tm, :]

    for j in range(D_FF // FFN_CHUNK):
        c0 = j * FFN_CHUNK
        a = jax.nn.gelu(conv_half(c0)) * conv_half(D_FF + c0)
        act_ref[:, c0:c0 + FFN_CHUNK] = a.astype(BF16)
    f = _dot(act_ref[...], wd_ref[...]) + bd_ref[...]
    y = DEEPNORM_ALPHA * xm32 + f
    o_ref[...] = _layer_norm(y, g_ref[...], b_ref[...])


def _ffn(x, seq, wu, bu, cw, cb, wd, bd, g, b):
    n = x.shape[0]
    tm = TOKEN_TILE
    vec = _const_spec((1, D_MODEL))
    return pl.pallas_call(
        functools.partial(_ffn_kernel, seq // tm),
        grid=(n // tm,),
        in_specs=_halo_specs(tm, n, D_MODEL) + [
            _const_spec((D_MODEL, 2 * D_FF)), _const_spec((1, 2 * D_FF)),
            _const_spec((3, 2 * D_FF)), _const_spec((1, 2 * D_FF)),
            _const_spec((D_FF, D_MODEL)), vec, vec, vec],
        out_specs=pl.BlockSpec((tm, D_MODEL), lambda i: (i, 0)),
        out_shape=jax.ShapeDtypeStruct((n, D_MODEL), F32),
        scratch_shapes=[pltpu.VMEM((tm, D_FF), BF16)],
        compiler_params=pltpu.CompilerParams(
            dimension_semantics=("parallel",), vmem_limit_bytes=VMEM_LIMIT),
        name="ffn",
    )(x, x, x, wu, bu, cw, cb, wd, bd, g, b)


def _trunk(x3, layers):
    batch, seq, _ = x3.shape
    assert seq % TOKEN_TILE == 0 and seq % (GRID_W * ATTN_ROWS_PER_STEP) == 0
    assert seq // GRID_W >= WIN_H
    x = x3.reshape(batch * seq, D_MODEL)
    for p in layers:
        q, k, v, yc, ga, gc = _proj(x, seq, p["w_in"], p["b_in"], p["sc_w"], p["sc_b"])
        ya = _attn(q, k, v, p["bias"], batch, seq)
        x = _merge(x, ya, yc, ga, gc, p["w_br_attn"], p["w_br_conv"], p["w_o"], p["b_o"],
                   p["ln1_g"], p["ln1_b"])
        x = _ffn(x, seq, p["w_up"], p["b_up"], p["f_w"], p["f_b"], p["w_down"], p["b_down"],
                 p["ln2_g"], p["ln2_b"])
    return x.reshape(batch, seq, D_MODEL)


def kernel(x_prompt, x_sample, w_in, b_in, attn_rpb, sc_conv_w, sc_conv_b, w_br_attn, w_br_conv,
           w_o, b_o, ln1_g, ln1_b, ffn_w_up, ffn_b_up, ffn_conv_w, ffn_conv_b, ffn_w_down,
           ffn_b_down, ln2_g, ln2_b):
    row = lambda a: a.reshape(1, -1).astype(F32)
    layers = []
    for l in range(DEPTH):
        layers.append(dict(
            w_in=w_in[l].astype(BF16), b_in=row(b_in[l]),
            sc_w=sc_conv_w[l].astype(F32), sc_b=row(sc_conv_b[l]),
            bias=_attn_bias_table(attn_rpb[l]),
            w_br_attn=w_br_attn[l].astype(BF16), w_br_conv=w_br_conv[l].astype(BF16),
            w_o=w_o[l].astype(BF16), b_o=row(b_o[l]),
            ln1_g=row(ln1_g[l]), ln1_b=row(ln1_b[l]),
            w_up=ffn_w_up[l].astype(BF16), b_up=row(ffn_b_up[l]),
            f_w=ffn_conv_w[l].astype(F32), f_b=row(ffn_conv_b[l]),
            w_down=ffn_w_down[l].astype(BF16), b_down=row(ffn_b_down[l]),
            ln2_g=row(ln2_g[l]), ln2_b=row(ln2_b[l])))
    return _trunk(x_prompt, layers), _trunk(x_sample, layers)
```

```python
import functools

import jax
import jax.numpy as jnp
import numpy as np
from jax import lax
from jax.experimental import pallas as pl
from jax.experimental.pallas import tpu as pltpu

D_MODEL = 1024
DEPTH = 2
GRID_W = 64
N_HEADS = 8
HEAD_DIM = 64
ATTN_WIDTH = N_HEADS * HEAD_DIM
CONV_WIDTH = D_MODEL - ATTN_WIDTH
WIN_H = 8
WIN_W = 16
D_FF = 2816
PROJ_WIDTH = 3 * ATTN_WIDTH + 3 * CONV_WIDTH + 2 * D_MODEL
DEEPNORM_ALPHA = (2 * DEPTH) ** 0.25
LN_EPS = 1e-5
ATTN_SCALE = HEAD_DIM ** -0.5

HALO = 8
TOKEN_TILE = 1024
FFN_TILE = 1024
EPILOGUE_ROWS = 256
FFN_CHUNK = 256
KEY_TOKENS = WIN_H * GRID_W
PAIR_LANES = 2 * HEAD_DIM
N_PAIRS = N_HEADS // 2
ATTN_ROWS_PER_STEP = 16
VMEM_LIMIT = 56 * 1024 * 1024

BF16 = jnp.bfloat16
F32 = jnp.float32


def _dot(a, b):
    return jnp.dot(a, b, preferred_element_type=F32)


def _layer_norm(y, g, b):
    mu = jnp.mean(y, axis=-1, keepdims=True)
    yc = y - mu
    var = jnp.mean(yc * yc, axis=-1, keepdims=True)
    return yc * lax.rsqrt(var + LN_EPS) * g + b


GELU_K1 = float(np.sqrt(2.0 / np.pi))
GELU_K2 = GELU_K1 * 0.044715


def _gelu_tanh_x2(x):
    u = x * (GELU_K1 + GELU_K2 * (x * x))
    return x + x * jnp.tanh(u)


def _seq_edge_flags(tile_idx, tiles_per_seq):
    pos = tile_idx % tiles_per_seq
    return pos == 0, pos == tiles_per_seq - 1


def _dwconv3_biased(h, bias, w_ref, conv_b, first, last):
    n = h.shape[0] - 2 * HALO
    neg_b = jnp.broadcast_to(-bias, (HALO, h.shape[1]))
    h = jnp.concatenate([jnp.where(first, neg_b, h[0:HALO]), h[HALO:HALO + n],
                         jnp.where(last, neg_b, h[HALO + n:])], axis=0)
    w0 = w_ref[0:1, :]
    w1 = w_ref[1:2, :]
    w2 = w_ref[2:3, :]
    const = bias * (w0 + w1 + w2) + conv_b
    up = pltpu.roll(h, 1, 0)
    dn = pltpu.roll(h, n + 2 * HALO - 1, 0)
    return (up * w0 + h * w1 + dn * w2 + const)[HALO:HALO + n, :]


def _dwconv3_rows(h, w_ref, b_ref):
    n = h.shape[0]
    up = pltpu.roll(h, 1, 0)
    dn = pltpu.roll(h, n - 1, 0)
    return up * w_ref[0:1, :] + h * w_ref[1:2, :] + dn * w_ref[2:3, :] + b_ref[...]


def _proj_kernel(tiles_per_seq, xm_ref, xp_ref, xn_ref, w_ref, b_ref, cw_ref, cb_ref,
                 q_ref, k_ref, v_ref, yc_ref, ga_ref, gc_ref):
    tm = xm_ref.shape[0]
    first, last = _seq_edge_flags(pl.program_id(0), tiles_per_seq)
    xm32 = xm_ref[...]
    xm = xm32.astype(BF16)
    a = ATTN_WIDTH
    c = CONV_WIDTH
    o = 3 * a + 3 * c
    for dst in (ga_ref, gc_ref):
        g = _dot(xm, w_ref[:, o:o + D_MODEL]) + b_ref[:, o:o + D_MODEL]
        dst[...] = jax.nn.sigmoid(g).astype(BF16)
        o += D_MODEL
    xe = jnp.concatenate([xp_ref[...], xm32, xn_ref[...]], axis=0).astype(BF16)
    o = 3 * a
    u = _dot(xe, w_ref[:, o:o + c]) + b_ref[:, o:o + c]
    gcin = _dot(xe, w_ref[:, o + 2 * c:o + 3 * c]) + b_ref[:, o + 2 * c:o + 3 * c]
    cu = gcin * u
    zero = jnp.zeros((HALO, c), F32)
    cu = jnp.concatenate([jnp.where(first, zero, cu[0:HALO]), cu[HALO:HALO + tm],
                          jnp.where(last, zero, cu[HALO + tm:])], axis=0)
    conv = _dwconv3_rows(cu, cw_ref, cb_ref)[HALO:HALO + tm, :]
    gb = _dot(xm, w_ref[:, o + c:o + 2 * c]) + b_ref[:, o + c:o + 2 * c]
    yc_ref[...] = (gb * conv).astype(BF16)
    qkv = _dot(xm, w_ref[:, 0:3 * a]) + b_ref[:, 0:3 * a]
    q_ref[...] = (qkv[:, 0:a] * ATTN_SCALE).astype(BF16)
    k_ref[...] = qkv[:, a:2 * a].astype(BF16)
    v_ref[...] = qkv[:, 2 * a:3 * a].astype(BF16)


def _const_spec(shape):
    return pl.BlockSpec(shape, lambda *_: (0,) * len(shape), pipeline_mode=pl.Buffered(1))


def _halo_specs(tm, n_tokens, width):
    per = tm // HALO
    nblk = n_tokens // HALO
    return [
        pl.BlockSpec((tm, width), lambda i: (i, 0)),
        pl.BlockSpec((HALO, width), lambda i: (jnp.maximum(i * per - 1, 0), 0)),
        pl.BlockSpec((HALO, width), lambda i: (jnp.minimum((i + 1) * per, nblk - 1), 0)),
    ]


def _proj(x, seq, w_in, b_in, cw, cb):
    n = x.shape[0]
    tm = TOKEN_TILE
    tok = lambda width: pl.BlockSpec((tm, width), lambda i: (i, 0))
    return pl.pallas_call(
        functools.partial(_proj_kernel, seq // tm),
        grid=(n // tm,),
        in_specs=_halo_specs(tm, n, D_MODEL) + [
            _const_spec((D_MODEL, PROJ_WIDTH)), _const_spec((1, PROJ_WIDTH)),
            _const_spec((3, CONV_WIDTH)), _const_spec((1, CONV_WIDTH))],
        out_specs=[tok(ATTN_WIDTH)] * 3 + [tok(CONV_WIDTH)] + [tok(D_MODEL)] * 2,
        out_shape=[jax.ShapeDtypeStruct((n, ATTN_WIDTH), BF16)] * 3
        + [jax.ShapeDtypeStruct((n, CONV_WIDTH), BF16)]
        + [jax.ShapeDtypeStruct((n, D_MODEL), BF16)] * 2,
        compiler_params=pltpu.CompilerParams(
            dimension_semantics=("parallel",), vmem_limit_bytes=VMEM_LIMIT),
        name="proj",
    )(x, x, x, w_in, b_in, cw, cb)


def _attn_row(rows, r, q_ref, k_ref, v_ref, bias_ref, o_ref, q_off):
    rs = jnp.clip(r - WIN_H // 2, 0, rows - WIN_H)
    d = r - rs
    start = pl.multiple_of(rs * GRID_W, GRID_W)
    low = lax.broadcasted_iota(jnp.int32, (GRID_W, PAIR_LANES), 1) < HEAD_DIM
    ones = jnp.ones((KEY_TOKENS, PAIR_LANES), BF16)
    qrow = pl.ds(q_off, GRID_W)
    scores = []
    for p in range(N_PAIRS):
        sl = slice(p * PAIR_LANES, (p + 1) * PAIR_LANES)
        qp = q_ref[qrow, sl]
        zero = jnp.zeros_like(qp)
        qm = jnp.concatenate([jnp.where(low, qp, zero), jnp.where(low, zero, qp)], axis=0)
        kp = k_ref[pl.ds(start, KEY_TOKENS), sl]
        s = lax.dot_general(qm, kp, (((1,), (1,)), ((), ())), preferred_element_type=F32)
        scores.append(s + bias_ref[d, p])
    probs = []
    for s in scores:
        m = jnp.max(s, axis=-1, keepdims=True)
        probs.append(jnp.exp(s - m).astype(BF16))
    for p in range(N_PAIRS):
        sl = slice(p * PAIR_LANES, (p + 1) * PAIR_LANES)
        vext = jnp.concatenate([v_ref[pl.ds(start, KEY_TOKENS), sl], ones], axis=1)
        ol = _dot(probs[p], vext)
        num = jnp.where(low, ol[0:GRID_W, 0:PAIR_LANES], ol[GRID_W:, 0:PAIR_LANES])
        den = jnp.where(low, ol[0:GRID_W, PAIR_LANES:], ol[GRID_W:, PAIR_LANES:])
        o_ref[qrow, sl] = (num / den).astype(BF16)


def _attn_kernel(rows, q_ref, k_ref, v_ref, bias_ref, o_ref):
    r0 = pl.program_id(1) * ATTN_ROWS_PER_STEP

    def body(j, carry):
        _attn_row(rows, r0 + j, q_ref, k_ref, v_ref, bias_ref, o_ref,
                  pl.multiple_of(j * GRID_W, GRID_W))
        return carry

    lax.fori_loop(0, ATTN_ROWS_PER_STEP, body, 0, unroll=4)


def _attn(q, k, v, bias, batch, seq):
    rows = seq // GRID_W
    n = q.shape[0]
    rb = ATTN_ROWS_PER_STEP
    blk = rows // rb
    qo_spec = pl.BlockSpec((rb * GRID_W, ATTN_WIDTH), lambda b, i: (b * blk + i, 0))
    kv_spec = pl.BlockSpec((seq, ATTN_WIDTH), lambda b, i: (b, 0))
    return pl.pallas_call(
        functools.partial(_attn_kernel, rows),
        grid=(batch, blk),
        in_specs=[qo_spec, kv_spec, kv_spec,
                  _const_spec((WIN_H, N_PAIRS, 2 * GRID_W, KEY_TOKENS))],
        out_specs=qo_spec,
        out_shape=jax.ShapeDtypeStruct((n, ATTN_WIDTH), BF16),
        compiler_params=pltpu.CompilerParams(
            dimension_semantics=("parallel", "arbitrary"), vmem_limit_bytes=VMEM_LIMIT),
        name="attn",
    )(q, k, v, bias)


def _attn_bias_table(rpb):
    qj = np.arange(GRID_W)[:, None]
    kj = np.arange(GRID_W)[None, :]
    dc_idx = np.clip(kj - qj + (WIN_W - 1), 0, 2 * WIN_W - 2)
    js = np.clip(qj - WIN_W // 2, 0, GRID_W - WIN_W)
    valid = (kj >= js) & (kj < js + WIN_W)
    onehot = (dc_idx[None] == np.arange(2 * WIN_W - 1)[:, None, None]).astype(np.float32)
    c = jnp.einsum("hdc,cqk->hqdk", rpb.astype(F32), onehot, precision=lax.Precision.HIGHEST)
    c = jnp.where(valid[None, :, None, :], c, -jnp.inf)
    tabs = [c[:, :, WIN_H - 1 - d:2 * WIN_H - 1 - d, :].reshape(N_HEADS, GRID_W, KEY_TOKENS)
            for d in range(WIN_H)]
    return jnp.stack(tabs).reshape(WIN_H, N_PAIRS, 2 * GRID_W, KEY_TOKENS)


def _merge_kernel(x_ref, ya_ref, yc_ref, ga_ref, gc_ref, wa_ref, wc_ref, wo_ref, bo_ref,
                  g_ref, b_ref, o_ref):
    for r0 in range(0, x_ref.shape[0], EPILOGUE_ROWS):
        rows = slice(r0, r0 + EPILOGUE_ROWS)
        merged = (ga_ref[rows, :].astype(F32) * _dot(ya_ref[rows, :], wa_ref[...])
                  + gc_ref[rows, :].astype(F32) * _dot(yc_ref[rows, :], wc_ref[...]))
        mix = _dot(merged.astype(BF16), wo_ref[...]) + bo_ref[...]
        y = DEEPNORM_ALPHA * x_ref[rows, :] + mix
        o_ref[rows, :] = _layer_norm(y, g_ref[...], b_ref[...])


def _merge(x, ya, yc, ga, gc, wa, wc, wo, bo, g, b):
    n = x.shape[0]
    tm = TOKEN_TILE
    tok = lambda width: pl.BlockSpec((tm, width), lambda i: (i, 0))
    vec = _const_spec((1, D_MODEL))
    return pl.pallas_call(
        _merge_kernel,
        grid=(n // tm,),
        in_specs=[tok(D_MODEL), tok(ATTN_WIDTH), tok(CONV_WIDTH), tok(D_MODEL), tok(D_MODEL),
                  _const_spec((ATTN_WIDTH, D_MODEL)), _const_spec((CONV_WIDTH, D_MODEL)),
                  _const_spec((D_MODEL, D_MODEL)), vec, vec, vec],
        out_specs=tok(D_MODEL),
        out_shape=jax.ShapeDtypeStruct((n, D_MODEL), F32),
        compiler_params=pltpu.CompilerParams(
            dimension_semantics=("parallel",), vmem_limit_bytes=VMEM_LIMIT),
        name="merge",
    )(x, ya, yc, ga, gc, wa, wc, wo, bo, g, b)


def _ffn_kernel(tiles_per_seq, xm_ref, xp_ref, xn_ref, wu_ref, bu_ref, cw_ref, cb_ref,
                wd_ref, bd_ref, g_ref, b_ref, o_ref, act_ref):
    tm = xm_ref.shape[0]
    first, last = _seq_edge_flags(pl.program_id(0), tiles_per_seq)
    xe = jnp.concatenate([xp_ref[...], xm_ref[...], xn_ref[...]], axis=0).astype(BF16)

    def conv_half(c0):
        sl = slice(c0, c0 + FFN_CHUNK)
        h = _dot(xe, wu_ref[:, sl])
        return _dwconv3_biased(h, bu_ref[:, sl], cw_ref.at[:, sl], cb_ref[:, sl], first, last)

    for j in range(D_FF // FFN_CHUNK):
        c0 = j * FFN_CHUNK
        a = _gelu_tanh_x2(conv_half(c0)) * conv_half(D_FF + c0)
        act_ref[:, c0:c0 + FFN_CHUNK] = a.astype(BF16)
    for r0 in range(0, tm, EPILOGUE_ROWS):
        rows = slice(r0, r0 + EPILOGUE_ROWS)
        f = _dot(act_ref[rows, :], wd_ref[...]) + bd_ref[...]
        y = DEEPNORM_ALPHA * xm_ref[rows, :] + f
        o_ref[rows, :] = _layer_norm(y, g_ref[...], b_ref[...])


def _ffn(x, seq, wu, bu, cw, cb, wd, bd, g, b):
    n = x.shape[0]
    tm = FFN_TILE
    vec = _const_spec((1, D_MODEL))
    return pl.pallas_call(
        functools.partial(_ffn_kernel, seq // tm),
        grid=(n // tm,),
        in_specs=_halo_specs(tm, n, D_MODEL) + [
            _const_spec((D_MODEL, 2 * D_FF)), _const_spec((1, 2 * D_FF)),
            _const_spec((3, 2 * D_FF)), _const_spec((1, 2 * D_FF)),
            _const_spec((D_FF, D_MODEL)), vec, vec, vec],
        out_specs=pl.BlockSpec((tm, D_MODEL), lambda i: (i, 0)),
        out_shape=jax.ShapeDtypeStruct((n, D_MODEL), F32),
        scratch_shapes=[pltpu.VMEM((tm, D_FF), BF16)],
        compiler_params=pltpu.CompilerParams(
            dimension_semantics=("parallel",), vmem_limit_bytes=VMEM_LIMIT),
        name="ffn",
    )(x, x, x, wu, bu, cw, cb, wd, bd, g, b)


def _trunk(x3, layers):
    batch, seq, _ = x3.shape
    assert seq % TOKEN_TILE == 0 and seq % FFN_TILE == 0
    assert seq % (GRID_W * ATTN_ROWS_PER_STEP) == 0
    assert seq // GRID_W >= WIN_H
    x = x3.reshape(batch * seq, D_MODEL)
    for p in layers:
        q, k, v, yc, ga, gc = _proj(x, seq, p["w_in"], p["b_in"], p["sc_w"], p["sc_b"])
        ya = _attn(q, k, v, p["bias"], batch, seq)
        x = _merge(x, ya, yc, ga, gc, p["w_br_attn"], p["w_br_conv"], p["w_o"], p["b_o"],
                   p["ln1_g"], p["ln1_b"])
        x = _ffn(x, seq, p["w_up"], p["b_up"], p["f_w"], p["f_b"], p["w_down"], p["b_down"],
                 p["ln2_g"], p["ln2_b"])
    return x.reshape(batch, seq, D_MODEL)


def kernel(x_prompt, x_sample, w_in, b_in, attn_rpb, sc_conv_w, sc_conv_b, w_br_attn, w_br_conv,
           w_o, b_o, ln1_g, ln1_b, ffn_w_up, ffn_b_up, ffn_conv_w, ffn_conv_b, ffn_w_down,
           ffn_b_down, ln2_g, ln2_b):
    row = lambda a: a.reshape(1, -1).astype(F32)
    layers = []
    for l in range(DEPTH):
        layers.append(dict(
            w_in=w_in[l].astype(BF16), b_in=row(b_in[l]),
            sc_w=sc_conv_w[l].astype(F32), sc_b=row(sc_conv_b[l]),
            bias=_attn_bias_table(attn_rpb[l]),
            w_br_attn=w_br_attn[l].astype(BF16), w_br_conv=w_br_conv[l].astype(BF16),
            w_o=w_o[l].astype(BF16), b_o=row(b_o[l]),
            ln1_g=row(ln1_g[l]), ln1_b=row(ln1_b[l]),
            w_up=ffn_w_up[l].astype(BF16), b_up=row(ffn_b_up[l]),
            f_w=ffn_conv_w[l].astype(F32), f_b=row(ffn_conv_b[l]),
            w_down=(0.5 * ffn_w_down[l]).astype(BF16), b_down=row(ffn_b_down[l]),
            ln2_g=row(ln2_g[l]), ln2_b=row(ln2_b[l])))
    return _trunk(x_prompt, layers), _trunk(x_sample, layers)
```

```python
import functools

import jax
import jax.numpy as jnp
import numpy as np
from jax import lax
from jax.experimental import pallas as pl
from jax.experimental.pallas import tpu as pltpu

D_MODEL = 1024
DEPTH = 2
GRID_W = 64
N_HEADS = 8
HEAD_DIM = 64
ATTN_WIDTH = N_HEADS * HEAD_DIM
CONV_WIDTH = D_MODEL - ATTN_WIDTH
WIN_H = 8
WIN_W = 16
D_FF = 2816
PROJ_WIDTH = 3 * ATTN_WIDTH + 3 * CONV_WIDTH + 2 * D_MODEL
DEEPNORM_ALPHA = (2 * DEPTH) ** 0.25
LN_EPS = 1e-5
ATTN_SCALE = HEAD_DIM ** -0.5

HALO = 8
TOKEN_TILE = 1024
FFN_TILE = 1024
EPILOGUE_ROWS = 256
FFN_CHUNK = 256
KEY_TOKENS = WIN_H * GRID_W
PAIR_LANES = 2 * HEAD_DIM
N_PAIRS = N_HEADS // 2
ATTN_ROWS_PER_STEP = 16
ATTN_SCORE_LOOKAHEAD = 1
VMEM_LIMIT = 56 * 1024 * 1024

BF16 = jnp.bfloat16
F32 = jnp.float32


def _dot(a, b):
    return jnp.dot(a, b, preferred_element_type=F32)


def _layer_norm(y, g, b):
    mu = jnp.mean(y, axis=-1, keepdims=True)
    yc = y - mu
    var = jnp.mean(yc * yc, axis=-1, keepdims=True)
    return yc * lax.rsqrt(var + LN_EPS) * g + b


GELU_K1 = float(np.sqrt(2.0 / np.pi))
GELU_K2 = GELU_K1 * 0.044715


def _gelu_tanh_x2(x):
    u = x * (GELU_K1 + GELU_K2 * (x * x))
    return x + x * jnp.tanh(u)


def _seq_edge_flags(tile_idx, tiles_per_seq):
    pos = tile_idx % tiles_per_seq
    return pos == 0, pos == tiles_per_seq - 1


def _dwconv3_biased(h, bias, w_ref, conv_b, first, last):
    n = h.shape[0] - 2 * HALO
    neg_b = jnp.broadcast_to(-bias, (HALO, h.shape[1]))
    h = jnp.concatenate([jnp.where(first, neg_b, h[0:HALO]), h[HALO:HALO + n],
                         jnp.where(last, neg_b, h[HALO + n:])], axis=0)
    w0 = w_ref[0:1, :]
    w1 = w_ref[1:2, :]
    w2 = w_ref[2:3, :]
    const = bias * (w0 + w1 + w2) + conv_b
    up = pltpu.roll(h, 1, 0)
    dn = pltpu.roll(h, n + 2 * HALO - 1, 0)
    return (up * w0 + h * w1 + dn * w2 + const)[HALO:HALO + n, :]


def _dwconv3_rows(h, w_ref, b_ref):
    n = h.shape[0]
    up = pltpu.roll(h, 1, 0)
    dn = pltpu.roll(h, n - 1, 0)
    return up * w_ref[0:1, :] + h * w_ref[1:2, :] + dn * w_ref[2:3, :] + b_ref[...]


def _proj_kernel(tiles_per_seq, xm_ref, xp_ref, xn_ref, w_ref, b_ref, cw_ref, cb_ref,
                 q_ref, k_ref, v_ref, yc_ref, ga_ref, gc_ref):
    tm = xm_ref.shape[0]
    first, last = _seq_edge_flags(pl.program_id(0), tiles_per_seq)
    xm32 = xm_ref[...]
    xm = xm32.astype(BF16)
    a = ATTN_WIDTH
    c = CONV_WIDTH
    o = 3 * a + 3 * c
    for dst in (ga_ref, gc_ref):
        g = _dot(xm, w_ref[:, o:o + D_MODEL]) + b_ref[:, o:o + D_MODEL]
        dst[...] = jax.nn.sigmoid(g).astype(BF16)
        o += D_MODEL
    xe = jnp.concatenate([xp_ref[...], xm32, xn_ref[...]], axis=0).astype(BF16)
    o = 3 * a
    u = _dot(xe, w_ref[:, o:o + c]) + b_ref[:, o:o + c]
    gcin = _dot(xe, w_ref[:, o + 2 * c:o + 3 * c]) + b_ref[:, o + 2 * c:o + 3 * c]
    cu = gcin * u
    zero = jnp.zeros((HALO, c), F32)
    cu = jnp.concatenate([jnp.where(first, zero, cu[0:HALO]), cu[HALO:HALO + tm],
                          jnp.where(last, zero, cu[HALO + tm:])], axis=0)
    conv = _dwconv3_rows(cu, cw_ref, cb_ref)[HALO:HALO + tm, :]
    gb = _dot(xm, w_ref[:, o + c:o + 2 * c]) + b_ref[:, o + c:o + 2 * c]
    yc_ref[...] = (gb * conv).astype(BF16)
    qkv = _dot(xm, w_ref[:, 0:3 * a]) + b_ref[:, 0:3 * a]
    q_ref[...] = (qkv[:, 0:a] * ATTN_SCALE).astype(BF16)
    k_ref[...] = qkv[:, a:2 * a].astype(BF16)
    v_ref[...] = qkv[:, 2 * a:3 * a].astype(BF16)


def _const_spec(shape):
    return pl.BlockSpec(shape, lambda *_: (0,) * len(shape), pipeline_mode=pl.Buffered(1))


def _halo_specs(tm, n_tokens, width):
    per = tm // HALO
    nblk = n_tokens // HALO
    return [
        pl.BlockSpec((tm, width), lambda i: (i, 0)),
        pl.BlockSpec((HALO, width), lambda i: (jnp.maximum(i * per - 1, 0), 0)),
        pl.BlockSpec((HALO, width), lambda i: (jnp.minimum((i + 1) * per, nblk - 1), 0)),
    ]


def _proj(x, seq, w_in, b_in, cw, cb):
    n = x.shape[0]
    tm = TOKEN_TILE
    tok = lambda width: pl.BlockSpec((tm, width), lambda i: (i, 0))
    return pl.pallas_call(
        functools.partial(_proj_kernel, seq // tm),
        grid=(n // tm,),
        in_specs=_halo_specs(tm, n, D_MODEL) + [
            _const_spec((D_MODEL, PROJ_WIDTH)), _const_spec((1, PROJ_WIDTH)),
            _const_spec((3, CONV_WIDTH)), _const_spec((1, CONV_WIDTH))],
        out_specs=[tok(ATTN_WIDTH)] * 3 + [tok(CONV_WIDTH)] + [tok(D_MODEL)] * 2,
        out_shape=[jax.ShapeDtypeStruct((n, ATTN_WIDTH), BF16)] * 3
        + [jax.ShapeDtypeStruct((n, CONV_WIDTH), BF16)]
        + [jax.ShapeDtypeStruct((n, D_MODEL), BF16)] * 2,
        compiler_params=pltpu.CompilerParams(
            dimension_semantics=("parallel",), vmem_limit_bytes=VMEM_LIMIT),
        name="proj",
    )(x, x, x, w_in, b_in, cw, cb)


def _pair_lanes(p):
    return slice(p * PAIR_LANES, (p + 1) * PAIR_LANES)


def _window_start(rows, r):
    rs = jnp.clip(r - WIN_H // 2, 0, rows - WIN_H)
    return pl.multiple_of(rs * GRID_W, GRID_W), r - rs


def _attn_scores(rows, r, j, q_ref, k_ref, bias_ref):
    start, d = _window_start(rows, r)
    low = lax.broadcasted_iota(jnp.int32, (GRID_W, PAIR_LANES), 1) < HEAD_DIM
    scores = []
    for p in range(N_PAIRS):
        qp = q_ref[j * GRID_W:(j + 1) * GRID_W, _pair_lanes(p)]
        zero = jnp.zeros_like(qp)
        qm = jnp.concatenate([jnp.where(low, qp, zero), jnp.where(low, zero, qp)], axis=0)
        kp = k_ref[pl.ds(start, KEY_TOKENS), _pair_lanes(p)]
        s = lax.dot_general(qm, kp, (((1,), (1,)), ((), ())), preferred_element_type=F32)
        scores.append(s + bias_ref[d, p])
    return scores


def _attn_finish(rows, r, j, scores, v_ref, o_ref):
    start, _ = _window_start(rows, r)
    low = lax.broadcasted_iota(jnp.int32, (GRID_W, PAIR_LANES), 1) < HEAD_DIM
    ones = jnp.ones((KEY_TOKENS, PAIR_LANES), BF16)
    probs = []
    for s in scores:
        m = jnp.max(s, axis=-1, keepdims=True)
        probs.append(jnp.exp(s - m).astype(BF16))
    for p in range(N_PAIRS):
        vext = jnp.concatenate([v_ref[pl.ds(start, KEY_TOKENS), _pair_lanes(p)], ones], axis=1)
        ol = _dot(probs[p], vext)
        num = jnp.where(low, ol[0:GRID_W, 0:PAIR_LANES], ol[GRID_W:, 0:PAIR_LANES])
        den = jnp.where(low, ol[0:GRID_W, PAIR_LANES:], ol[GRID_W:, PAIR_LANES:])
        o_ref[j * GRID_W:(j + 1) * GRID_W, _pair_lanes(p)] = (num / den).astype(BF16)


def _attn_kernel(rows, q_ref, k_ref, v_ref, bias_ref, o_ref):
    r0 = pl.program_id(1) * ATTN_ROWS_PER_STEP
    ahead = ATTN_SCORE_LOOKAHEAD
    pending = [_attn_scores(rows, r0 + j, j, q_ref, k_ref, bias_ref) for j in range(ahead)]
    for j in range(ATTN_ROWS_PER_STEP):
        if j + ahead < ATTN_ROWS_PER_STEP:
            pending.append(_attn_scores(rows, r0 + j + ahead, j + ahead, q_ref, k_ref, bias_ref))
        _attn_finish(rows, r0 + j, j, pending.pop(0), v_ref, o_ref)


def _attn(q, k, v, bias, batch, seq):
    rows = seq // GRID_W
    n = q.shape[0]
    rb = ATTN_ROWS_PER_STEP
    blk = rows // rb
    qo_spec = pl.BlockSpec((rb * GRID_W, ATTN_WIDTH), lambda b, i: (b * blk + i, 0))
    kv_spec = pl.BlockSpec((seq, ATTN_WIDTH), lambda b, i: (b, 0))
    return pl.pallas_call(
        functools.partial(_attn_kernel, rows),
        grid=(batch, blk),
        in_specs=[qo_spec, kv_spec, kv_spec,
                  _const_spec((WIN_H, N_PAIRS, 2 * GRID_W, KEY_TOKENS))],
        out_specs=qo_spec,
        out_shape=jax.ShapeDtypeStruct((n, ATTN_WIDTH), BF16),
        compiler_params=pltpu.CompilerParams(
            dimension_semantics=("parallel", "arbitrary"), vmem_limit_bytes=VMEM_LIMIT),
        name="attn",
    )(q, k, v, bias)


def _attn_bias_table(rpb):
    qj = np.arange(GRID_W)[:, None]
    kj = np.arange(GRID_W)[None, :]
    dc_idx = np.clip(kj - qj + (WIN_W - 1), 0, 2 * WIN_W - 2)
    js = np.clip(qj - WIN_W // 2, 0, GRID_W - WIN_W)
    valid = (kj >= js) & (kj < js + WIN_W)
    onehot = (dc_idx[None] == np.arange(2 * WIN_W - 1)[:, None, None]).astype(np.float32)
    c = jnp.einsum("hdc,cqk->hqdk", rpb.astype(F32), onehot, precision=lax.Precision.HIGHEST)
    c = jnp.where(valid[None, :, None, :], c, -jnp.inf)
    tabs = [c[:, :, WIN_H - 1 - d:2 * WIN_H - 1 - d, :].reshape(N_HEADS, GRID_W, KEY_TOKENS)
            for d in range(WIN_H)]
    return jnp.stack(tabs).reshape(WIN_H, N_PAIRS, 2 * GRID_W, KEY_TOKENS)


def _merge_kernel(x_ref, ya_ref, yc_ref, ga_ref, gc_ref, wa_ref, wc_ref, wo_ref, bo_ref,
                  g_ref, b_ref, o_ref):
    for r0 in range(0, x_ref.shape[0], EPILOGUE_ROWS):
        rows = slice(r0, r0 + EPILOGUE_ROWS)
        merged = (ga_ref[rows, :].astype(F32) * _dot(ya_ref[rows, :], wa_ref[...])
                  + gc_ref[rows, :].astype(F32) * _dot(yc_ref[rows, :], wc_ref[...]))
        mix = _dot(merged.astype(BF16), wo_ref[...]) + bo_ref[...]
        y = DEEPNORM_ALPHA * x_ref[rows, :] + mix
        o_ref[rows, :] = _layer_norm(y, g_ref[...], b_ref[...])


def _merge(x, ya, yc, ga, gc, wa, wc, wo, bo, g, b):
    n = x.shape[0]
    tm = TOKEN_TILE
    tok = lambda width: pl.BlockSpec((tm, width), lambda i: (i, 0))
    vec = _const_spec((1, D_MODEL))
    return pl.pallas_call(
        _merge_kernel,
        grid=(n // tm,),
        in_specs=[tok(D_MODEL), tok(ATTN_WIDTH), tok(CONV_WIDTH), tok(D_MODEL), tok(D_MODEL),
                  _const_spec((ATTN_WIDTH, D_MODEL)), _const_spec((CONV_WIDTH, D_MODEL)),
                  _const_spec((D_MODEL, D_MODEL)), vec, vec, vec],
        out_specs=tok(D_MODEL),
        out_shape=jax.ShapeDtypeStruct((n, D_MODEL), F32),
        compiler_params=pltpu.CompilerParams(
            dimension_semantics=("parallel",), vmem_limit_bytes=VMEM_LIMIT),
        name="merge",
    )(x, ya, yc, ga, gc, wa, wc, wo, bo, g, b)


def _ffn_kernel(tiles_per_seq, xm_ref, xp_ref, xn_ref, wu_ref, bu_ref, cw_ref, cb_ref,
                wd_ref, bd_ref, g_ref, b_ref, o_ref, act_ref):
    tm = xm_ref.shape[0]
    first, last = _seq_edge_flags(pl.program_id(0), tiles_per_seq)
    xe = jnp.concatenate([xp_ref[...], xm_ref[...], xn_ref[...]], axis=0).astype(BF16)

    def conv_half(c0):
        sl = slice(c0, c0 + FFN_CHUNK)
        h = _dot(xe, wu_ref[:, sl])
        return _dwconv3_biased(h, bu_ref[:, sl], cw_ref.at[:, sl], cb_ref[:, sl], first, last)

    for j in range(D_FF // FFN_CHUNK):
        c0 = j * FFN_CHUNK
        a = _gelu_tanh_x2(conv_half(c0)) * conv_half(D_FF + c0)
        act_ref[:, c0:c0 + FFN_CHUNK] = a.astype(BF16)
    for r0 in range(0, tm, EPILOGUE_ROWS):
        rows = slice(r0, r0 + EPILOGUE_ROWS)
        f = _dot(act_ref[rows, :], wd_ref[...]) + bd_ref[...]
        y = DEEPNORM_ALPHA * xm_ref[rows, :] + f
        o_ref[rows, :] = _layer_norm(y, g_ref[...], b_ref[...])


def _ffn(x, seq, wu, bu, cw, cb, wd, bd, g, b):
    n = x.shape[0]
    tm = FFN_TILE
    vec = _const_spec((1, D_MODEL))
    return pl.pallas_call(
        functools.partial(_ffn_kernel, seq // tm),
        grid=(n // tm,),
        in_specs=_halo_specs(tm, n, D_MODEL) + [
            _const_spec((D_MODEL, 2 * D_FF)), _const_spec((1, 2 * D_FF)),
            _const_spec((3, 2 * D_FF)), _const_spec((1, 2 * D_FF)),
            _const_spec((D_FF, D_MODEL)), vec, vec, vec],
        out_specs=pl.BlockSpec((tm, D_MODEL), lambda i: (i, 0)),
        out_shape=jax.ShapeDtypeStruct((n, D_MODEL), F32),
        scratch_shapes=[pltpu.VMEM((tm, D_FF), BF16)],
        compiler_params=pltpu.CompilerParams(
            dimension_semantics=("parallel",), vmem_limit_bytes=VMEM_LIMIT),
        name="ffn",
    )(x, x, x, wu, bu, cw, cb, wd, bd, g, b)


def _trunk(x3, layers):
    batch, seq, _ = x3.shape
    assert seq % TOKEN_TILE == 0 and seq % FFN_TILE == 0
    assert seq % (GRID_W * ATTN_ROWS_PER_STEP) == 0
    assert seq // GRID_W >= WIN_H
    x = x3.reshape(batch * seq, D_MODEL)
    for p in layers:
        q, k, v, yc, ga, gc = _proj(x, seq, p["w_in"], p["b_in"], p["sc_w"], p["sc_b"])
        ya = _attn(q, k, v, p["bias"], batch, seq)
        x = _merge(x, ya, yc, ga, gc, p["w_br_attn"], p["w_br_conv"], p["w_o"], p["b_o"],
                   p["ln1_g"], p["ln1_b"])
        x = _ffn(x, seq, p["w_up"], p["b_up"], p["f_w"], p["f_b"], p["w_down"], p["b_down"],
                 p["ln2_g"], p["ln2_b"])
    return x.reshape(batch, seq, D_MODEL)


def kernel(x_prompt, x_sample, w_in, b_in, attn_rpb, sc_conv_w, sc_conv_b, w_br_attn, w_br_conv,
           w_o, b_o, ln1_g, ln1_b, ffn_w_up, ffn_b_up, ffn_conv_w, ffn_conv_b, ffn_w_down,
           ffn_b_down, ln2_g, ln2_b):
    row = lambda a: a.reshape(1, -1).astype(F32)
    layers = []
    for l in range(DEPTH):
        layers.append(dict(
            w_in=w_in[l].astype(BF16), b_in=row(b_in[l]),
            sc_w=sc_conv_w[l].astype(F32), sc_b=row(sc_conv_b[l]),
            bias=_attn_bias_table(attn_rpb[l]),
            w_br_attn=w_br_attn[l].astype(BF16), w_br_conv=w_br_conv[l].astype(BF16),
            w_o=w_o[l].astype(BF16), b_o=row(b_o[l]),
            ln1_g=row(ln1_g[l]), ln1_b=row(ln1_b[l]),
            w_up=ffn_w_up[l].astype(BF16), b_up=row(ffn_b_up[l]),
            f_w=ffn_conv_w[l].astype(F32), f_b=row(ffn_conv_b[l]),
            w_down=(0.5 * ffn_w_down[l]).astype(BF16), b_down=row(ffn_b_down[l]),
            ln2_g=row(ln2_g[l]), ln2_b=row(ln2_b[l])))
    return _trunk(x_prompt, layers), _trunk(x_sample, layers)
```

```python
import functools

import jax
import jax.numpy as jnp
import numpy as np
from jax import lax
from jax.experimental import pallas as pl
from jax.experimental.pallas import tpu as pltpu

D_MODEL = 1024
DEPTH = 2
GRID_W = 64
N_HEADS = 8
HEAD_DIM = 64
ATTN_WIDTH = N_HEADS * HEAD_DIM
CONV_WIDTH = D_MODEL - ATTN_WIDTH
WIN_H = 8
WIN_W = 16
D_FF = 2816
PROJ_WIDTH = 3 * ATTN_WIDTH + 3 * CONV_WIDTH + 2 * D_MODEL
DEEPNORM_ALPHA = (2 * DEPTH) ** 0.25
LN_EPS = 1e-5
ATTN_SCALE = HEAD_DIM ** -0.5

HALO = 8
TOKEN_TILE = 1024
FFN_TILE = 1024
EPILOGUE_ROWS = 256
FFN_CHUNK = 256
ROW_CLASSES = HALO
KEY_TOKENS = WIN_H * GRID_W
PAIR_LANES = 2 * HEAD_DIM
N_PAIRS = N_HEADS // 2
ATTN_ROWS_PER_STEP = 16
ATTN_SCORE_LOOKAHEAD = 1
VMEM_LIMIT = 56 * 1024 * 1024

BF16 = jnp.bfloat16
F32 = jnp.float32


def _dot(a, b):
    return jnp.dot(a, b, preferred_element_type=F32)


def _layer_norm(y, g, b):
    mu = jnp.mean(y, axis=-1, keepdims=True)
    yc = y - mu
    var = jnp.mean(yc * yc, axis=-1, keepdims=True)
    return yc * lax.rsqrt(var + LN_EPS) * g + b


GELU_K1 = float(np.sqrt(2.0 / np.pi))
GELU_K2 = GELU_K1 * 0.044715


def _gelu_tanh_x2(x):
    u = x * (GELU_K1 + GELU_K2 * (x * x))
    return x + x * jnp.tanh(u)


def _seq_edge_flags(tile_idx, tiles_per_seq):
    pos = tile_idx % tiles_per_seq
    return pos == 0, pos == tiles_per_seq - 1


def _dwconv3_rows(h, w_ref, b_ref):
    n = h.shape[0]
    up = pltpu.roll(h, 1, 0)
    dn = pltpu.roll(h, n - 1, 0)
    return up * w_ref[0:1, :] + h * w_ref[1:2, :] + dn * w_ref[2:3, :] + b_ref[...]


def _proj_kernel(tiles_per_seq, xm_ref, xp_ref, xn_ref, w_ref, b_ref, cw_ref, cb_ref,
                 q_ref, k_ref, v_ref, yc_ref, ga_ref, gc_ref):
    tm = xm_ref.shape[0]
    first, last = _seq_edge_flags(pl.program_id(0), tiles_per_seq)
    xm32 = xm_ref[...]
    xm = xm32.astype(BF16)
    a = ATTN_WIDTH
    c = CONV_WIDTH
    o = 3 * a + 3 * c
    for dst in (ga_ref, gc_ref):
        g = _dot(xm, w_ref[:, o:o + D_MODEL]) + b_ref[:, o:o + D_MODEL]
        dst[...] = jax.nn.sigmoid(g).astype(BF16)
        o += D_MODEL
    xe = jnp.concatenate([xp_ref[...], xm32, xn_ref[...]], axis=0).astype(BF16)
    o = 3 * a
    u = _dot(xe, w_ref[:, o:o + c]) + b_ref[:, o:o + c]
    gcin = _dot(xe, w_ref[:, o + 2 * c:o + 3 * c]) + b_ref[:, o + 2 * c:o + 3 * c]
    cu = gcin * u
    zero = jnp.zeros((HALO, c), F32)
    cu = jnp.concatenate([jnp.where(first, zero, cu[0:HALO]), cu[HALO:HALO + tm],
                          jnp.where(last, zero, cu[HALO + tm:])], axis=0)
    conv = _dwconv3_rows(cu, cw_ref, cb_ref)[HALO:HALO + tm, :]
    gb = _dot(xm, w_ref[:, o + c:o + 2 * c]) + b_ref[:, o + c:o + 2 * c]
    yc_ref[...] = (gb * conv).astype(BF16)
    qkv = _dot(xm, w_ref[:, 0:3 * a]) + b_ref[:, 0:3 * a]
    q_ref[...] = (qkv[:, 0:a] * ATTN_SCALE).astype(BF16)
    k_ref[...] = qkv[:, a:2 * a].astype(BF16)
    v_ref[...] = qkv[:, 2 * a:3 * a].astype(BF16)


def _const_spec(shape):
    return pl.BlockSpec(shape, lambda *_: (0,) * len(shape), pipeline_mode=pl.Buffered(1))


def _halo_specs(tm, n_tokens, width):
    per = tm // HALO
    nblk = n_tokens // HALO
    return [
        pl.BlockSpec((tm, width), lambda i: (i, 0)),
        pl.BlockSpec((HALO, width), lambda i: (jnp.maximum(i * per - 1, 0), 0)),
        pl.BlockSpec((HALO, width), lambda i: (jnp.minimum((i + 1) * per, nblk - 1), 0)),
    ]


def _proj(x, seq, w_in, b_in, cw, cb):
    n = x.shape[0]
    tm = TOKEN_TILE
    tok = lambda width: pl.BlockSpec((tm, width), lambda i: (i, 0))
    return pl.pallas_call(
        functools.partial(_proj_kernel, seq // tm),
        grid=(n // tm,),
        in_specs=_halo_specs(tm, n, D_MODEL) + [
            _const_spec((D_MODEL, PROJ_WIDTH)), _const_spec((1, PROJ_WIDTH)),
            _const_spec((3, CONV_WIDTH)), _const_spec((1, CONV_WIDTH))],
        out_specs=[tok(ATTN_WIDTH)] * 3 + [tok(CONV_WIDTH)] + [tok(D_MODEL)] * 2,
        out_shape=[jax.ShapeDtypeStruct((n, ATTN_WIDTH), BF16)] * 3
        + [jax.ShapeDtypeStruct((n, CONV_WIDTH), BF16)]
        + [jax.ShapeDtypeStruct((n, D_MODEL), BF16)] * 2,
        compiler_params=pltpu.CompilerParams(
            dimension_semantics=("parallel",), vmem_limit_bytes=VMEM_LIMIT),
        name="proj",
    )(x, x, x, w_in, b_in, cw, cb)


def _pair_lanes(p):
    return slice(p * PAIR_LANES, (p + 1) * PAIR_LANES)


def _window_start(rows, r):
    rs = jnp.clip(r - WIN_H // 2, 0, rows - WIN_H)
    return pl.multiple_of(rs * GRID_W, GRID_W), r - rs


def _attn_scores(rows, r, j, q_ref, k_ref, bias_ref):
    start, d = _window_start(rows, r)
    low = lax.broadcasted_iota(jnp.int32, (GRID_W, PAIR_LANES), 1) < HEAD_DIM
    scores = []
    for p in range(N_PAIRS):
        qp = q_ref[j * GRID_W:(j + 1) * GRID_W, _pair_lanes(p)]
        zero = jnp.zeros_like(qp)
        qm = jnp.concatenate([jnp.where(low, qp, zero), jnp.where(low, zero, qp)], axis=0)
        kp = k_ref[pl.ds(start, KEY_TOKENS), _pair_lanes(p)]
        s = lax.dot_general(qm, kp, (((1,), (1,)), ((), ())), preferred_element_type=F32)
        scores.append(s + bias_ref[d, p])
    return scores


def _attn_finish(rows, r, j, scores, v_ref, o_ref):
    start, _ = _window_start(rows, r)
    low = lax.broadcasted_iota(jnp.int32, (GRID_W, PAIR_LANES), 1) < HEAD_DIM
    ones = jnp.ones((KEY_TOKENS, PAIR_LANES), BF16)
    probs = []
    for s in scores:
        m = jnp.max(s, axis=-1, keepdims=True)
        probs.append(jnp.exp(s - m).astype(BF16))
    for p in range(N_PAIRS):
        vext = jnp.concatenate([v_ref[pl.ds(start, KEY_TOKENS), _pair_lanes(p)], ones], axis=1)
        ol = _dot(probs[p], vext)
        num = jnp.where(low, ol[0:GRID_W, 0:PAIR_LANES], ol[GRID_W:, 0:PAIR_LANES])
        den = jnp.where(low, ol[0:GRID_W, PAIR_LANES:], ol[GRID_W:, PAIR_LANES:])
        o_ref[j * GRID_W:(j + 1) * GRID_W, _pair_lanes(p)] = (num / den).astype(BF16)


def _attn_kernel(rows, q_ref, k_ref, v_ref, bias_ref, o_ref):
    r0 = pl.program_id(1) * ATTN_ROWS_PER_STEP
    ahead = ATTN_SCORE_LOOKAHEAD
    pending = [_attn_scores(rows, r0 + j, j, q_ref, k_ref, bias_ref) for j in range(ahead)]
    for j in range(ATTN_ROWS_PER_STEP):
        if j + ahead < ATTN_ROWS_PER_STEP:
            pending.append(_attn_scores(rows, r0 + j + ahead, j + ahead, q_ref, k_ref, bias_ref))
        _attn_finish(rows, r0 + j, j, pending.pop(0), v_ref, o_ref)


def _attn(q, k, v, bias, batch, seq):
    rows = seq // GRID_W
    n = q.shape[0]
    rb = ATTN_ROWS_PER_STEP
    blk = rows // rb
    qo_spec = pl.BlockSpec((rb * GRID_W, ATTN_WIDTH), lambda b, i: (b * blk + i, 0))
    kv_spec = pl.BlockSpec((seq, ATTN_WIDTH), lambda b, i: (b, 0))
    return pl.pallas_call(
        functools.partial(_attn_kernel, rows),
        grid=(batch, blk),
        in_specs=[qo_spec, kv_spec, kv_spec,
                  _const_spec((WIN_H, N_PAIRS, 2 * GRID_W, KEY_TOKENS))],
        out_specs=qo_spec,
        out_shape=jax.ShapeDtypeStruct((n, ATTN_WIDTH), BF16),
        compiler_params=pltpu.CompilerParams(
            dimension_semantics=("parallel", "arbitrary"), vmem_limit_bytes=VMEM_LIMIT),
        name="attn",
    )(q, k, v, bias)


def _attn_bias_table(rpb):
    qj = np.arange(GRID_W)[:, None]
    kj = np.arange(GRID_W)[None, :]
    dc_idx = np.clip(kj - qj + (WIN_W - 1), 0, 2 * WIN_W - 2)
    js = np.clip(qj - WIN_W // 2, 0, GRID_W - WIN_W)
    valid = (kj >= js) & (kj < js + WIN_W)
    onehot = (dc_idx[None] == np.arange(2 * WIN_W - 1)[:, None, None]).astype(np.float32)
    c = jnp.einsum("hdc,cqk->hqdk", rpb.astype(F32), onehot, precision=lax.Precision.HIGHEST)
    c = jnp.where(valid[None, :, None, :], c, -jnp.inf)
    tabs = [c[:, :, WIN_H - 1 - d:2 * WIN_H - 1 - d, :].reshape(N_HEADS, GRID_W, KEY_TOKENS)
            for d in range(WIN_H)]
    return jnp.stack(tabs).reshape(WIN_H, N_PAIRS, 2 * GRID_W, KEY_TOKENS)


def _merge_kernel(x_ref, ya_ref, yc_ref, ga_ref, gc_ref, wa_ref, wc_ref, wo_ref, bo_ref,
                  g_ref, b_ref, o_ref):
    for r0 in range(0, x_ref.shape[0], EPILOGUE_ROWS):
        rows = slice(r0, r0 + EPILOGUE_ROWS)
        merged = (ga_ref[rows, :].astype(F32) * _dot(ya_ref[rows, :], wa_ref[...])
                  + gc_ref[rows, :].astype(F32) * _dot(yc_ref[rows, :], wc_ref[...]))
        mix = _dot(merged.astype(BF16), wo_ref[...]) + bo_ref[...]
        y = DEEPNORM_ALPHA * x_ref[rows, :] + mix
        o_ref[rows, :] = _layer_norm(y, g_ref[...], b_ref[...])


def _merge(x, ya, yc, ga, gc, wa, wc, wo, bo, g, b):
    n = x.shape[0]
    tm = TOKEN_TILE
    tok = lambda width: pl.BlockSpec((tm, width), lambda i: (i, 0))
    vec = _const_spec((1, D_MODEL))
    return pl.pallas_call(
        _merge_kernel,
        grid=(n // tm,),
        in_specs=[tok(D_MODEL), tok(ATTN_WIDTH), tok(CONV_WIDTH), tok(D_MODEL), tok(D_MODEL),
                  _const_spec((ATTN_WIDTH, D_MODEL)), _const_spec((CONV_WIDTH, D_MODEL)),
                  _const_spec((D_MODEL, D_MODEL)), vec, vec, vec],
        out_specs=tok(D_MODEL),
        out_shape=jax.ShapeDtypeStruct((n, D_MODEL), F32),
        compiler_params=pltpu.CompilerParams(
            dimension_semantics=("parallel",), vmem_limit_bytes=VMEM_LIMIT),
        name="merge",
    )(x, ya, yc, ga, gc, wa, wc, wo, bo, g, b)


def _ffn_kernel(tiles_per_seq, xm_ref, xp_ref, xn_ref, wu_ref, bu_ref, cw_ref, cb_ref,
                wd_ref, bd_ref, g_ref, b_ref, o_ref, act_ref):
    tm = xm_ref.shape[0]
    cs = tm // ROW_CLASSES
    first, last = _seq_edge_flags(pl.program_id(0), tiles_per_seq)

    x_cm = jnp.swapaxes(xm_ref[...].reshape(cs, ROW_CLASSES, D_MODEL), 0, 1).reshape(tm, D_MODEL)
    xe = jnp.concatenate([x_cm, xp_ref[...], xn_ref[...]], axis=0).astype(BF16)
    row0 = lax.broadcasted_iota(jnp.int32, (HALO, 1), 0) == 0
    row7 = lax.broadcasted_iota(jnp.int32, (HALO, 1), 0) == HALO - 1

    def conv_half(c0):
        sl = slice(c0, c0 + FFN_CHUNK)
        h = _dot(xe, wu_ref[:, sl])
        bu = bu_ref[:, sl]
        w0 = cw_ref[0:1, sl]
        w1 = cw_ref[1:2, sl]
        w2 = cw_ref[2:3, sl]
        const = bu * (w0 + w1 + w2) + cb_ref[:, sl]
        cls = [h[c * cs:(c + 1) * cs] for c in range(ROW_CLASSES)]
        before = jnp.where(first, -bu, h[tm + HALO - 1:tm + HALO])
        after = jnp.where(last, -bu, h[tm + HALO:tm + HALO + 1])
        up0 = pltpu.roll(cls[ROW_CLASSES - 1], 1, 0)
        up0 = jnp.concatenate([jnp.where(row0, before, up0[0:HALO]), up0[HALO:]], axis=0)
        dn7 = pltpu.roll(cls[0], cs - 1, 0)
        dn7 = jnp.concatenate([dn7[:cs - HALO], jnp.where(row7, after, dn7[cs - HALO:])], axis=0)
        ups = [up0] + cls[:-1]
        dns = cls[1:] + [dn7]
        return jnp.concatenate(
            [ups[c] * w0 + cls[c] * w1 + dns[c] * w2 + const for c in range(ROW_CLASSES)], axis=0)

    for j in range(D_FF // FFN_CHUNK):
        c0 = j * FFN_CHUNK
        a = _gelu_tanh_x2(conv_half(c0)) * conv_half(D_FF + c0)
        act_ref[:, c0:c0 + FFN_CHUNK] = a.astype(BF16)
    outs = []
    for r0 in range(0, tm, EPILOGUE_ROWS):
        rows = slice(r0, r0 + EPILOGUE_ROWS)
        f = _dot(act_ref[rows, :], wd_ref[...]) + bd_ref[...]
        outs.append(_layer_norm(DEEPNORM_ALPHA * x_cm[rows] + f, g_ref[...], b_ref[...]))
    out_cm = jnp.concatenate(outs, axis=0).reshape(ROW_CLASSES, cs, D_MODEL)
    o_ref[...] = jnp.swapaxes(out_cm, 0, 1).reshape(tm, D_MODEL)


def _ffn(x, seq, wu, bu, cw, cb, wd, bd, g, b):
    n = x.shape[0]
    tm = FFN_TILE
    vec = _const_spec((1, D_MODEL))
    return pl.pallas_call(
        functools.partial(_ffn_kernel, seq // tm),
        grid=(n // tm,),
        in_specs=_halo_specs(tm, n, D_MODEL) + [
            _const_spec((D_MODEL, 2 * D_FF)), _const_spec((1, 2 * D_FF)),
            _const_spec((3, 2 * D_FF)), _const_spec((1, 2 * D_FF)),
            _const_spec((D_FF, D_MODEL)), vec, vec, vec],
        out_specs=pl.BlockSpec((tm, D_MODEL), lambda i: (i, 0)),
        out_shape=jax.ShapeDtypeStruct((n, D_MODEL), F32),
        scratch_shapes=[pltpu.VMEM((tm, D_FF), BF16)],
        compiler_params=pltpu.CompilerParams(
            dimension_semantics=("parallel",), vmem_limit_bytes=VMEM_LIMIT),
        name="ffn",
    )(x, x, x, wu, bu, cw, cb, wd, bd, g, b)


def _trunk(x3, layers):
    batch, seq, _ = x3.shape
    assert seq % TOKEN_TILE == 0 and seq % FFN_TILE == 0
    assert seq % (GRID_W * ATTN_ROWS_PER_STEP) == 0
    assert seq // GRID_W >= WIN_H
    x = x3.reshape(batch * seq, D_MODEL)
    for p in layers:
        q, k, v, yc, ga, gc = _proj(x, seq, p["w_in"], p["b_in"], p["sc_w"], p["sc_b"])
        ya = _attn(q, k, v, p["bias"], batch, seq)
        x = _merge(x, ya, yc, ga, gc, p["w_br_attn"], p["w_br_conv"], p["w_o"], p["b_o"],
                   p["ln1_g"], p["ln1_b"])
        x = _ffn(x, seq, p["w_up"], p["b_up"], p["f_w"], p["f_b"], p["w_down"], p["b_down"],
                 p["ln2_g"], p["ln2_b"])
    return x.reshape(batch, seq, D_MODEL)


def kernel(x_prompt, x_sample, w_in, b_in, attn_rpb, sc_conv_w, sc_conv_b, w_br_attn, w_br_conv,
           w_o, b_o, ln1_g, ln1_b, ffn_w_up, ffn_b_up, ffn_conv_w, ffn_conv_b, ffn_w_down,
           ffn_b_down, ln2_g, ln2_b):
    row = lambda a: a.reshape(1, -1).astype(F32)
    layers = []
    for l in range(DEPTH):
        layers.append(dict(
            w_in=w_in[l].astype(BF16), b_in=row(b_in[l]),
            sc_w=sc_conv_w[l].astype(F32), sc_b=row(sc_conv_b[l]),
            bias=_attn_bias_table(attn_rpb[l]),
            w_br_attn=w_br_attn[l].astype(BF16), w_br_conv=w_br_conv[l].astype(BF16),
            w_o=w_o[l].astype(BF16), b_o=row(b_o[l]),
            ln1_g=row(ln1_g[l]), ln1_b=row(ln1_b[l]),
            w_up=ffn_w_up[l].astype(BF16), b_up=row(ffn_b_up[l]),
            f_w=ffn_conv_w[l].astype(F32), f_b=row(ffn_conv_b[l]),
            w_down=(0.5 * ffn_w_down[l]).astype(BF16), b_down=row(ffn_b_down[l]),
            ln2_g=row(ln2_g[l]), ln2_b=row(ln2_b[l])))
    return _trunk(x_prompt, layers), _trunk(x_sample, layers)
```

```python
import functools

import jax
import jax.numpy as jnp
import numpy as np
from jax import lax
from jax.experimental import pallas as pl
from jax.experimental.pallas import tpu as pltpu

D_MODEL = 1024
DEPTH = 2
GRID_W = 64
N_HEADS = 8
HEAD_DIM = 64
ATTN_WIDTH = N_HEADS * HEAD_DIM
CONV_WIDTH = D_MODEL - ATTN_WIDTH
WIN_H = 8
WIN_W = 16
D_FF = 2816
PROJ_WIDTH = 3 * ATTN_WIDTH + 3 * CONV_WIDTH + 2 * D_MODEL
DEEPNORM_ALPHA = (2 * DEPTH) ** 0.25
LN_EPS = 1e-5
LOG2_E = float(np.log2(np.e))
ATTN_SCALE = HEAD_DIM ** -0.5 * LOG2_E

HALO = 8
TOKEN_TILE = 1024
FFN_TILE = 1024
EPILOGUE_ROWS = 256
FFN_CHUNK = 256
ROW_CLASSES = HALO
KEY_TOKENS = WIN_H * GRID_W
PAIR_LANES = 2 * HEAD_DIM
N_PAIRS = N_HEADS // 2
ATTN_ROWS_PER_STEP = 16
ATTN_SCORE_LOOKAHEAD = 1
VMEM_LIMIT = 56 * 1024 * 1024

BF16 = jnp.bfloat16
F32 = jnp.float32


def _dot(a, b):
    return jnp.dot(a, b, preferred_element_type=F32)


def _layer_norm(y, g, b):
    mu = jnp.mean(y, axis=-1, keepdims=True)
    yc = y - mu
    var = jnp.mean(yc * yc, axis=-1, keepdims=True)
    return yc * lax.rsqrt(var + LN_EPS) * g + b


GELU_K1 = float(np.sqrt(2.0 / np.pi))
GELU_K2 = GELU_K1 * 0.044715


def _gelu_tanh_x2(x):
    u = x * (GELU_K1 + GELU_K2 * (x * x))
    return x + x * jnp.tanh(u)


def _seq_edge_flags(tile_idx, tiles_per_seq):
    pos = tile_idx % tiles_per_seq
    return pos == 0, pos == tiles_per_seq - 1


def _dwconv3_rows(h, w_ref, b_ref):
    n = h.shape[0]
    up = pltpu.roll(h, 1, 0)
    dn = pltpu.roll(h, n - 1, 0)
    return up * w_ref[0:1, :] + h * w_ref[1:2, :] + dn * w_ref[2:3, :] + b_ref[...]


def _proj_kernel(tiles_per_seq, xm_ref, xp_ref, xn_ref, w_ref, b_ref, cw_ref, cb_ref,
                 q_ref, k_ref, v_ref, yc_ref, ga_ref, gc_ref):
    tm = xm_ref.shape[0]
    first, last = _seq_edge_flags(pl.program_id(0), tiles_per_seq)
    xm32 = xm_ref[...]
    xm = xm32.astype(BF16)
    a = ATTN_WIDTH
    c = CONV_WIDTH
    o = 3 * a + 3 * c
    for dst in (ga_ref, gc_ref):
        g = _dot(xm, w_ref[:, o:o + D_MODEL]) + b_ref[:, o:o + D_MODEL]
        dst[...] = jax.nn.sigmoid(g).astype(BF16)
        o += D_MODEL
    xe = jnp.concatenate([xp_ref[...], xm32, xn_ref[...]], axis=0).astype(BF16)
    o = 3 * a
    u = _dot(xe, w_ref[:, o:o + c]) + b_ref[:, o:o + c]
    gcin = _dot(xe, w_ref[:, o + 2 * c:o + 3 * c]) + b_ref[:, o + 2 * c:o + 3 * c]
    cu = gcin * u
    zero = jnp.zeros((HALO, c), F32)
    cu = jnp.concatenate([jnp.where(first, zero, cu[0:HALO]), cu[HALO:HALO + tm],
                          jnp.where(last, zero, cu[HALO + tm:])], axis=0)
    conv = _dwconv3_rows(cu, cw_ref, cb_ref)[HALO:HALO + tm, :]
    gb = _dot(xm, w_ref[:, o + c:o + 2 * c]) + b_ref[:, o + c:o + 2 * c]
    yc_ref[...] = (gb * conv).astype(BF16)
    qkv = _dot(xm, w_ref[:, 0:3 * a]) + b_ref[:, 0:3 * a]
    q_ref[...] = (qkv[:, 0:a] * ATTN_SCALE).astype(BF16)
    k_ref[...] = qkv[:, a:2 * a].astype(BF16)
    v_ref[...] = qkv[:, 2 * a:3 * a].astype(BF16)


def _const_spec(shape):
    return pl.BlockSpec(shape, lambda *_: (0,) * len(shape), pipeline_mode=pl.Buffered(1))


def _halo_specs(tm, n_tokens, width):
    per = tm // HALO
    nblk = n_tokens // HALO
    return [
        pl.BlockSpec((tm, width), lambda i: (i, 0)),
        pl.BlockSpec((HALO, width), lambda i: (jnp.maximum(i * per - 1, 0), 0)),
        pl.BlockSpec((HALO, width), lambda i: (jnp.minimum((i + 1) * per, nblk - 1), 0)),
    ]


def _proj(x, seq, w_in, b_in, cw, cb):
    n = x.shape[0]
    tm = TOKEN_TILE
    tok = lambda width: pl.BlockSpec((tm, width), lambda i: (i, 0))
    return pl.pallas_call(
        functools.partial(_proj_kernel, seq // tm),
        grid=(n // tm,),
        in_specs=_halo_specs(tm, n, D_MODEL) + [
            _const_spec((D_MODEL, PROJ_WIDTH)), _const_spec((1, PROJ_WIDTH)),
            _const_spec((3, CONV_WIDTH)), _const_spec((1, CONV_WIDTH))],
        out_specs=[tok(ATTN_WIDTH)] * 3 + [tok(CONV_WIDTH)] + [tok(D_MODEL)] * 2,
        out_shape=[jax.ShapeDtypeStruct((n, ATTN_WIDTH), BF16)] * 3
        + [jax.ShapeDtypeStruct((n, CONV_WIDTH), BF16)]
        + [jax.ShapeDtypeStruct((n, D_MODEL), BF16)] * 2,
        compiler_params=pltpu.CompilerParams(
            dimension_semantics=("parallel",), vmem_limit_bytes=VMEM_LIMIT),
        name="proj",
    )(x, x, x, w_in, b_in, cw, cb)


def _pair_lanes(p):
    return slice(p * PAIR_LANES, (p + 1) * PAIR_LANES)


def _window_start(rows, r):
    rs = jnp.clip(r - WIN_H // 2, 0, rows - WIN_H)
    return pl.multiple_of(rs * GRID_W, GRID_W), r - rs


def _attn_scores(rows, r, j, q_ref, k_ref, bias_ref):
    start, d = _window_start(rows, r)
    low = lax.broadcasted_iota(jnp.int32, (GRID_W, PAIR_LANES), 1) < HEAD_DIM
    scores = []
    for p in range(N_PAIRS):
        qp = q_ref[j * GRID_W:(j + 1) * GRID_W, _pair_lanes(p)]
        zero = jnp.zeros_like(qp)
        qm = jnp.concatenate([jnp.where(low, qp, zero), jnp.where(low, zero, qp)], axis=0)
        kp = k_ref[pl.ds(start, KEY_TOKENS), _pair_lanes(p)]
        s = lax.dot_general(qm, kp, (((1,), (1,)), ((), ())), preferred_element_type=F32)
        scores.append(s + bias_ref[d, p])
    return scores


def _attn_finish(rows, r, j, scores, v_ref, o_ref):
    start, _ = _window_start(rows, r)
    low = lax.broadcasted_iota(jnp.int32, (GRID_W, PAIR_LANES), 1) < HEAD_DIM
    ones = jnp.ones((KEY_TOKENS, PAIR_LANES), BF16)
    probs = []
    for s in scores:
        m = jnp.max(s, axis=-1, keepdims=True)
        probs.append(jnp.exp2(s - m).astype(BF16))
    for p in range(N_PAIRS):
        vext = jnp.concatenate([v_ref[pl.ds(start, KEY_TOKENS), _pair_lanes(p)], ones], axis=1)
        ol = _dot(probs[p], vext)
        num = jnp.where(low, ol[0:GRID_W, 0:PAIR_LANES], ol[GRID_W:, 0:PAIR_LANES])
        den = jnp.where(low, ol[0:GRID_W, PAIR_LANES:], ol[GRID_W:, PAIR_LANES:])
        o_ref[j * GRID_W:(j + 1) * GRID_W, _pair_lanes(p)] = (num / den).astype(BF16)


def _attn_kernel(rows, q_ref, k_ref, v_ref, bias_ref, o_ref):
    r0 = pl.program_id(1) * ATTN_ROWS_PER_STEP
    ahead = ATTN_SCORE_LOOKAHEAD
    pending = [_attn_scores(rows, r0 + j, j, q_ref, k_ref, bias_ref) for j in range(ahead)]
    for j in range(ATTN_ROWS_PER_STEP):
        if j + ahead < ATTN_ROWS_PER_STEP:
            pending.append(_attn_scores(rows, r0 + j + ahead, j + ahead, q_ref, k_ref, bias_ref))
        _attn_finish(rows, r0 + j, j, pending.pop(0), v_ref, o_ref)


def _attn(q, k, v, bias, batch, seq):
    rows = seq // GRID_W
    n = q.shape[0]
    rb = ATTN_ROWS_PER_STEP
    blk = rows // rb
    qo_spec = pl.BlockSpec((rb * GRID_W, ATTN_WIDTH), lambda b, i: (b * blk + i, 0))
    kv_spec = pl.BlockSpec((seq, ATTN_WIDTH), lambda b, i: (b, 0))
    return pl.pallas_call(
        functools.partial(_attn_kernel, rows),
        grid=(batch, blk),
        in_specs=[qo_spec, kv_spec, kv_spec,
                  _const_spec((WIN_H, N_PAIRS, 2 * GRID_W, KEY_TOKENS))],
        out_specs=qo_spec,
        out_shape=jax.ShapeDtypeStruct((n, ATTN_WIDTH), BF16),
        compiler_params=pltpu.CompilerParams(
            dimension_semantics=("parallel", "arbitrary"), vmem_limit_bytes=VMEM_LIMIT),
        name="attn",
    )(q, k, v, bias)


def _attn_bias_table(rpb):
    qj = np.arange(GRID_W)[:, None]
    kj = np.arange(GRID_W)[None, :]
    dc_idx = np.clip(kj - qj + (WIN_W - 1), 0, 2 * WIN_W - 2)
    js = np.clip(qj - WIN_W // 2, 0, GRID_W - WIN_W)
    valid = (kj >= js) & (kj < js + WIN_W)
    onehot = (dc_idx[None] == np.arange(2 * WIN_W - 1)[:, None, None]).astype(np.float32)
    c = jnp.einsum("hdc,cqk->hqdk", rpb.astype(F32), onehot, precision=lax.Precision.HIGHEST)
    c = jnp.where(valid[None, :, None, :], c * LOG2_E, -jnp.inf)
    tabs = [c[:, :, WIN_H - 1 - d:2 * WIN_H - 1 - d, :].reshape(N_HEADS, GRID_W, KEY_TOKENS)
            for d in range(WIN_H)]
    return jnp.stack(tabs).reshape(WIN_H, N_PAIRS, 2 * GRID_W, KEY_TOKENS)


def _merge_kernel(x_ref, ya_ref, yc_ref, ga_ref, gc_ref, wa_ref, wc_ref, wo_ref, bo_ref,
                  g_ref, b_ref, o_ref):
    for r0 in range(0, x_ref.shape[0], EPILOGUE_ROWS):
        rows = slice(r0, r0 + EPILOGUE_ROWS)
        merged = (ga_ref[rows, :].astype(F32) * _dot(ya_ref[rows, :], wa_ref[...])
                  + gc_ref[rows, :].astype(F32) * _dot(yc_ref[rows, :], wc_ref[...]))
        mix = _dot(merged.astype(BF16), wo_ref[...]) + bo_ref[...]
        y = DEEPNORM_ALPHA * x_ref[rows, :] + mix
        o_ref[rows, :] = _layer_norm(y, g_ref[...], b_ref[...])


def _merge(x, ya, yc, ga, gc, wa, wc, wo, bo, g, b):
    n = x.shape[0]
    tm = TOKEN_TILE
    tok = lambda width: pl.BlockSpec((tm, width), lambda i: (i, 0))
    vec = _const_spec((1, D_MODEL))
    return pl.pallas_call(
        _merge_kernel,
        grid=(n // tm,),
        in_specs=[tok(D_MODEL), tok(ATTN_WIDTH), tok(CONV_WIDTH), tok(D_MODEL), tok(D_MODEL),
                  _const_spec((ATTN_WIDTH, D_MODEL)), _const_spec((CONV_WIDTH, D_MODEL)),
                  _const_spec((D_MODEL, D_MODEL)), vec, vec, vec],
        out_specs=tok(D_MODEL),
        out_shape=jax.ShapeDtypeStruct((n, D_MODEL), F32),
        compiler_params=pltpu.CompilerParams(
            dimension_semantics=("parallel",), vmem_limit_bytes=VMEM_LIMIT),
        name="merge",
    )(x, ya, yc, ga, gc, wa, wc, wo, bo, g, b)


def _ffn_kernel(tiles_per_seq, xm_ref, xp_ref, xn_ref, wu_ref, bu_ref, cw_ref, cb_ref,
                wd_ref, bd_ref, g_ref, b_ref, o_ref, act_ref):
    tm = xm_ref.shape[0]
    cs = tm // ROW_CLASSES
    first, last = _seq_edge_flags(pl.program_id(0), tiles_per_seq)

    x_cm = jnp.swapaxes(xm_ref[...].reshape(cs, ROW_CLASSES, D_MODEL), 0, 1).reshape(tm, D_MODEL)
    xe = jnp.concatenate([x_cm, xp_ref[...], xn_ref[...]], axis=0).astype(BF16)
    row0 = lax.broadcasted_iota(jnp.int32, (HALO, 1), 0) == 0
    row7 = lax.broadcasted_iota(jnp.int32, (HALO, 1), 0) == HALO - 1

    def conv_half(c0):
        sl = slice(c0, c0 + FFN_CHUNK)
        h = _dot(xe, wu_ref[:, sl])
        bu = bu_ref[:, sl]
        w0 = cw_ref[0:1, sl]
        w1 = cw_ref[1:2, sl]
        w2 = cw_ref[2:3, sl]
        const = bu * (w0 + w1 + w2) + cb_ref[:, sl]
        cls = [h[c * cs:(c + 1) * cs] for c in range(ROW_CLASSES)]
        before = jnp.where(first, -bu, h[tm + HALO - 1:tm + HALO])
        after = jnp.where(last, -bu, h[tm + HALO:tm + HALO + 1])
        up0 = pltpu.roll(cls[ROW_CLASSES - 1], 1, 0)
        up0 = jnp.concatenate([jnp.where(row0, before, up0[0:HALO]), up0[HALO:]], axis=0)
        dn7 = pltpu.roll(cls[0], cs - 1, 0)
        dn7 = jnp.concatenate([dn7[:cs - HALO], jnp.where(row7, after, dn7[cs - HALO:])], axis=0)
        ups = [up0] + cls[:-1]
        dns = cls[1:] + [dn7]
        return jnp.concatenate(
            [ups[c] * w0 + cls[c] * w1 + dns[c] * w2 + const for c in range(ROW_CLASSES)], axis=0)

    for j in range(D_FF // FFN_CHUNK):
        c0 = j * FFN_CHUNK
        a = _gelu_tanh_x2(conv_half(c0)) * conv_half(D_FF + c0)
        act_ref[:, c0:c0 + FFN_CHUNK] = a.astype(BF16)
    outs = []
    for r0 in range(0, tm, EPILOGUE_ROWS):
        rows = slice(r0, r0 + EPILOGUE_ROWS)
        f = _dot(act_ref[rows, :], wd_ref[...]) + bd_ref[...]
        outs.append(_layer_norm(DEEPNORM_ALPHA * x_cm[rows] + f, g_ref[...], b_ref[...]))
    out_cm = jnp.concatenate(outs, axis=0).reshape(ROW_CLASSES, cs, D_MODEL)
    o_ref[...] = jnp.swapaxes(out_cm, 0, 1).reshape(tm, D_MODEL)


def _ffn(x, seq, wu, bu, cw, cb, wd, bd, g, b):
    n = x.shape[0]
    tm = FFN_TILE
    vec = _const_spec((1, D_MODEL))
    return pl.pallas_call(
        functools.partial(_ffn_kernel, seq // tm),
        grid=(n // tm,),
        in_specs=_halo_specs(tm, n, D_MODEL) + [
            _const_spec((D_MODEL, 2 * D_FF)), _const_spec((1, 2 * D_FF)),
            _const_spec((3, 2 * D_FF)), _const_spec((1, 2 * D_FF)),
            _const_spec((D_FF, D_MODEL)), vec, vec, vec],
        out_specs=pl.BlockSpec((tm, D_MODEL), lambda i: (i, 0)),
        out_shape=jax.ShapeDtypeStruct((n, D_MODEL), F32),
        scratch_shapes=[pltpu.VMEM((tm, D_FF), BF16)],
        compiler_params=pltpu.CompilerParams(
            dimension_semantics=("parallel",), vmem_limit_bytes=VMEM_LIMIT),
        name="ffn",
    )(x, x, x, wu, bu, cw, cb, wd, bd, g, b)


def _trunk(x3, layers):
    batch, seq, _ = x3.shape
    assert seq % TOKEN_TILE == 0 and seq % FFN_TILE == 0
    assert seq % (GRID_W * ATTN_ROWS_PER_STEP) == 0
    assert seq // GRID_W >= WIN_H
    x = x3.reshape(batch * seq, D_MODEL)
    for p in layers:
        q, k, v, yc, ga, gc = _proj(x, seq, p["w_in"], p["b_in"], p["sc_w"], p["sc_b"])
        ya = _attn(q, k, v, p["bias"], batch, seq)
        x = _merge(x, ya, yc, ga, gc, p["w_br_attn"], p["w_br_conv"], p["w_o"], p["b_o"],
                   p["ln1_g"], p["ln1_b"])
        x = _ffn(x, seq, p["w_up"], p["b_up"], p["f_w"], p["f_b"], p["w_down"], p["b_down"],
                 p["ln2_g"], p["ln2_b"])
    return x.reshape(batch, seq, D_MODEL)


def kernel(x_prompt, x_sample, w_in, b_in, attn_rpb, sc_conv_w, sc_conv_b, w_br_attn, w_br_conv,
           w_o, b_o, ln1_g, ln1_b, ffn_w_up, ffn_b_up, ffn_conv_w, ffn_conv_b, ffn_w_down,
           ffn_b_down, ln2_g, ln2_b):
    row = lambda a: a.reshape(1, -1).astype(F32)
    layers = []
    for l in range(DEPTH):
        layers.append(dict(
            w_in=w_in[l].astype(BF16), b_in=row(b_in[l]),
            sc_w=sc_conv_w[l].astype(F32), sc_b=row(sc_conv_b[l]),
            bias=_attn_bias_table(attn_rpb[l]),
            w_br_attn=w_br_attn[l].astype(BF16), w_br_conv=w_br_conv[l].astype(BF16),
            w_o=w_o[l].astype(BF16), b_o=row(b_o[l]),
            ln1_g=row(ln1_g[l]), ln1_b=row(ln1_b[l]),
            w_up=ffn_w_up[l].astype(BF16), b_up=row(ffn_b_up[l]),
            f_w=ffn_conv_w[l].astype(F32), f_b=row(ffn_conv_b[l]),
            w_down=(0.5 * ffn_w_down[l]).astype(BF16), b_down=row(ffn_b_down[l]),
            ln2_g=row(ln2_g[l]), ln2_b=row(ln2_b[l])))
    return _trunk(x_prompt, layers), _trunk(x_sample, layers)
```
